```python
import math
import jax, jax.numpy as jnp
from jax import lax
import numpy as np

D_MODEL = 1024
BATCH = 16
SEQ = 2048
DEPTH = 4

N_META = 16
EXPAND = 2
D_MIX = EXPAND * D_MODEL
D_MLSTM = D_MIX // 2
D_DIFF = D_MIX - D_MLSTM
MLSTM_HEADS = 4
MLSTM_HEAD_DIM = D_MLSTM // MLSTM_HEADS
MLSTM_CHUNK = 64
CONV_WIDTH = 4
DIFF_HEADS = 8
DIFF_HEAD_DIM = D_DIFF // DIFF_HEADS // 2
DIFF_V_DIM = 2 * DIFF_HEAD_DIM
Q_BLOCK = 128
D_IN = 5 * D_MLSTM + 2 * MLSTM_HEADS + 4 * D_DIFF
EPS = 1e-6
NEG = -1e30

kernel_name = 'hymba_mlstm_diffattn_hybrid'


def rmsnorm(x, g):
    xf = x.astype(jnp.float32)
    xf = xf * lax.rsqrt(jnp.mean(xf * xf, axis=-1, keepdims=True) + EPS)
    return (xf * g.astype(jnp.float32)).astype(x.dtype)


def head_rmsnorm(h, g, n_heads):
    B, L, D = h.shape
    hf = h.astype(jnp.float32).reshape(B, L, n_heads, D // n_heads)
    hf = hf * lax.rsqrt(jnp.mean(hf * hf, axis=-1, keepdims=True) + EPS)
    return (hf.reshape(B, L, D) * g.astype(jnp.float32)).astype(h.dtype)


def causal_conv(x, w, b):
    C = x.shape[-1]
    y = lax.conv_general_dilated(
        x, w[:, None, :].astype(x.dtype), window_strides=(1,),
        padding=[(CONV_WIDTH - 1, 0)], dimension_numbers=('NWC', 'WIO', 'NWC'),
        feature_group_count=C)
    return y + b.astype(x.dtype)


def mlstm_chunkwise(q, k, v, log_i, log_f):
    B, L, H, dh = q.shape
    pad = MLSTM_CHUNK - N_META
    pw = ((0, 0), (pad, 0), (0, 0), (0, 0))
    q = jnp.pad(q, pw) * (dh ** -0.5)
    k = jnp.pad(k, pw)
    v = jnp.pad(v, pw)
    log_i = jnp.pad(log_i, ((0, 0), (pad, 0), (0, 0)), constant_values=NEG)
    log_f = jnp.pad(log_f, ((0, 0), (pad, 0), (0, 0)))
    Lp = L + pad
    nc = Lp // MLSTM_CHUNK

    def to_chunks(a):
        a = a.reshape((B, nc, MLSTM_CHUNK, H) + a.shape[3:])
        return jnp.moveaxis(a, (1, 3), (0, 2))

    causal = jnp.tril(jnp.ones((MLSTM_CHUNK, MLSTM_CHUNK), dtype=bool))

    def step(carry, inp):
        C, n, m = carry
        qc, kc, vc, li, lf = inp
        qf = qc.astype(jnp.float32)
        kf = kc.astype(jnp.float32)
        vf = vc.astype(jnp.float32)
        b = jnp.cumsum(lf, axis=-1)
        D = b[..., :, None] - b[..., None, :] + li[..., None, :]
        D = jnp.where(causal, D, NEG)
        inter = b + m[..., None]
        m_row = jnp.maximum(inter, jnp.max(D, axis=-1))
        w_intra = jnp.exp(D - m_row[..., None])
        w_inter = jnp.exp(inter - m_row)
        s = jnp.einsum('bhtd,bhsd->bhts', qf, kf) * w_intra
        num = (jnp.einsum('bhts,bhsd->bhtd', s, vf)
               + w_inter[..., None] * jnp.einsum('bhvk,bhtk->bhtv', C, qf))
        den = jnp.sum(s, axis=-1) + w_inter * jnp.einsum('bhk,bhtk->bht', n, qf)
        h = num / jnp.maximum(jnp.abs(den), jnp.exp(-m_row))[..., None]
        b_T = b[..., -1]
        g = b_T[..., None] - b + li
        m_new = jnp.maximum(b_T + m, jnp.max(g, axis=-1))
        wk = jnp.exp(g - m_new[..., None])
        decay = jnp.exp(b_T + m - m_new)
        C_new = decay[..., None, None] * C + jnp.einsum('bhs,bhsv,bhsk->bhvk', wk, vf, kf)
        n_new = decay[..., None] * n + jnp.einsum('bhs,bhsk->bhk', wk, kf)
        return (C_new, n_new, m_new), h

    init = (jnp.zeros((B, H, dh, dh), jnp.float32),
            jnp.zeros((B, H, dh), jnp.float32),
            jnp.zeros((B, H), jnp.float32))
    xs = (to_chunks(q), to_chunks(k), to_chunks(v), to_chunks(log_i), to_chunks(log_f))
    _, h = lax.scan(step, init, xs)
    h = jnp.moveaxis(h, (0, 2), (1, 3)).reshape(B, Lp, H, dh)
    return h[:, pad:]


def diff_attention(q, k, v, lam):
    B, L, H = q.shape[:3]
    scale = DIFF_HEAD_DIM ** -0.5
    kpos = jnp.arange(L)

    def attend(qb, qpos):
        s = jnp.einsum('bqhcd,bkhcd->bhcqk', qb, k).astype(jnp.float32) * scale
        s = jnp.where(kpos[None, :] <= qpos[:, None], s, NEG)
        p = jax.nn.softmax(s, axis=-1)
        a = p[:, :, 0] - lam * p[:, :, 1]
        return jnp.einsum('bhqk,bkhv->bqhv', a.astype(v.dtype), v)

    out_meta = attend(q[:, :N_META], jnp.arange(N_META))
    n_real = L - N_META
    nb = n_real // Q_BLOCK
    q_blocks = jnp.moveaxis(
        q[:, N_META:].reshape(B, nb, Q_BLOCK, H, 2, DIFF_HEAD_DIM), 1, 0)
    pos_blocks = (N_META + jnp.arange(n_real)).reshape(nb, Q_BLOCK)
    out_real = lax.map(lambda a: attend(a[0], a[1]), (q_blocks, pos_blocks))
    out_real = jnp.moveaxis(out_real, 0, 1).reshape(B, n_real, H, DIFF_V_DIM)
    return jnp.concatenate([out_meta, out_real], axis=1)


def hybrid_layer(x, layer, g_pre, g_post, w_in, b_gates, conv_w, conv_b, g_mlstm,
                 lq1, lk1, lq2, lk2, g_diff, w_out):
    B, L, _ = x.shape
    h = rmsnorm(x, g_pre)
    proj = jnp.einsum('bld,de->ble', h, w_in)
    splits = list(np.cumsum([D_MLSTM] * 5 + [MLSTM_HEADS] * 2 + [D_DIFF] * 3))
    q_m, k_m, v_m, o_m, z_m, i_m, f_m, q_d, k_d, v_d, z_d = jnp.split(proj, splits, axis=-1)

    qk = jax.nn.silu(causal_conv(jnp.concatenate([q_m, k_m], axis=-1), conv_w, conv_b))
    q_m, k_m = jnp.split(qk, 2, axis=-1)
    heads = lambda a: a.reshape(B, L, MLSTM_HEADS, MLSTM_HEAD_DIM)
    gates = (jnp.concatenate([i_m, f_m], axis=-1) + b_gates).astype(jnp.float32)
    log_i, f_pre = jnp.split(gates, 2, axis=-1)
    log_f = jax.nn.log_sigmoid(f_pre)
    hm = mlstm_chunkwise(heads(q_m), heads(k_m), heads(v_m), log_i, log_f)
    hm = hm.reshape(B, L, D_MLSTM).astype(x.dtype) * jax.nn.sigmoid(o_m)
    hm = head_rmsnorm(hm, g_mlstm, MLSTM_HEADS) * jax.nn.silu(z_m)

    lam_init = 0.8 - 0.6 * math.exp(-0.3 * layer)
    lam = (jnp.exp(jnp.sum(lq1.astype(jnp.float32) * lk1.astype(jnp.float32)))
           - jnp.exp(jnp.sum(lq2.astype(jnp.float32) * lk2.astype(jnp.float32))) + lam_init)
    qd = q_d.reshape(B, L, DIFF_HEADS, 2, DIFF_HEAD_DIM)
    kd = k_d.reshape(B, L, DIFF_HEADS, 2, DIFF_HEAD_DIM)
    vd = v_d.reshape(B, L, DIFF_HEADS, DIFF_V_DIM)
    hd = diff_attention(qd, kd, vd, lam).reshape(B, L, D_DIFF)
    hd = head_rmsnorm(hd, g_diff, DIFF_HEADS) * (1.0 - lam_init) * jax.nn.silu(z_d)

    y = jnp.einsum('ble,ed->bld', jnp.concatenate([hm, hd], axis=-1), w_out)
    return x + rmsnorm(y, g_post)


def setup_inputs(seed: int = 0) -> dict:
    key = jax.random.key(seed)
    ks = jax.random.split(key, 16)
    nrm = jax.random.normal
    x = nrm(ks[0], (BATCH, SEQ, D_MODEL), jnp.float32)
    meta_tokens = nrm(ks[1], (N_META, D_MODEL), jnp.float32)
    pre_norm_g = 1.0 + 0.02 * nrm(ks[2], (DEPTH, D_MODEL), jnp.float32)
    post_norm_g = 1.0 + 0.02 * nrm(ks[3], (DEPTH, D_MODEL), jnp.float32)
    w_in = nrm(ks[4], (DEPTH, D_MODEL, D_IN), jnp.float32) * D_MODEL ** -0.5
    b_i = 0.1 * nrm(ks[5], (DEPTH, MLSTM_HEADS), jnp.float32)
    b_f = jnp.linspace(3.0, 6.0, MLSTM_HEADS, dtype=jnp.float32)[None, :] \
        + 0.1 * nrm(ks[6], (DEPTH, MLSTM_HEADS), jnp.float32)
    b_gates = jnp.concatenate([b_i, b_f], axis=-1)
    conv_w = nrm(ks[7], (DEPTH, CONV_WIDTH, 2 * D_MLSTM), jnp.float32) * CONV_WIDTH ** -0.5
    conv_b = 0.01 * nrm(ks[8], (DEPTH, 2 * D_MLSTM), jnp.float32)
    mlstm_norm_g = 1.0 + 0.02 * nrm(ks[9], (DEPTH, D_MLSTM), jnp.float32)
    lambda_q1 = 0.1 * nrm(ks[10], (DEPTH, DIFF_HEAD_DIM), jnp.float32)
    lambda_k1 = 0.1 * nrm(ks[11], (DEPTH, DIFF_HEAD_DIM), jnp.float32)
    lambda_q2 = 0.1 * nrm(ks[12], (DEPTH, DIFF_HEAD_DIM), jnp.float32)
    lambda_k2 = 0.1 * nrm(ks[13], (DEPTH, DIFF_HEAD_DIM), jnp.float32)
    diff_norm_g = 1.0 + 0.02 * nrm(ks[14], (DEPTH, D_DIFF), jnp.float32)
    w_out = nrm(ks[15], (DEPTH, D_MIX, D_MODEL), jnp.float32) * D_MIX ** -0.5
    return {'x': x, 'meta_tokens': meta_tokens, 'pre_norm_g': pre_norm_g,
            'post_norm_g': post_norm_g, 'w_in': w_in, 'b_gates': b_gates,
            'conv_w': conv_w, 'conv_b': conv_b, 'mlstm_norm_g': mlstm_norm_g,
            'lambda_q1': lambda_q1, 'lambda_k1': lambda_k1, 'lambda_q2': lambda_q2,
            'lambda_k2': lambda_k2, 'diff_norm_g': diff_norm_g, 'w_out': w_out}


def reference(x, meta_tokens, pre_norm_g, post_norm_g, w_in, b_gates, conv_w, conv_b,
              mlstm_norm_g, lambda_q1, lambda_k1, lambda_q2, lambda_k2, diff_norm_g, w_out):
    B = x.shape[0]
    meta = jnp.broadcast_to(meta_tokens[None].astype(x.dtype), (B, N_META, D_MODEL))
    h = jnp.concatenate([meta, x], axis=1)
    for layer in range(DEPTH):
        h = hybrid_layer(h, layer, pre_norm_g[layer], post_norm_g[layer], w_in[layer],
                         b_gates[layer], conv_w[layer], conv_b[layer], mlstm_norm_g[layer],
                         lambda_q1[layer], lambda_k1[layer], lambda_q2[layer],
                         lambda_k2[layer], diff_norm_g[layer], w_out[layer])
    return h[:, N_META:]
```

```python
import functools
import math

import jax
import jax.numpy as jnp
from jax import lax
from jax.experimental import pallas as pl
from jax.experimental.pallas import tpu as pltpu

D_MODEL = 1024
N_META = 16
D_MLSTM = 1024
D_DIFF = 1024
MLSTM_HEADS = 4
MLSTM_HEAD_DIM = 256
CONV_WIDTH = 4
DIFF_HEADS = 8
DIFF_HEAD_DIM = 64
DIFF_V_DIM = 128
N_GATES = 2 * MLSTM_HEADS
D_MAIN = 5 * D_MLSTM + 4 * D_DIFF
EPS = 1e-6
NEG = -1e30

LANES = 128
GATE_PAD = LANES
MLSTM_CHUNK = 256
ATTN_TQ = 128
ATTN_TK = 256
HALO = 16
VMEM_LIMIT = 48 * 1024 * 1024

F32 = jnp.float32
BF16 = jnp.bfloat16


def _row_tile(rows, cap=768):
    best = None
    for t in range(16, cap + 1, 16):
        if rows % t == 0:
            best = t
    assert best is not None, rows
    return best


def _sigmoid(x):
    return 1.0 / (1.0 + jnp.exp(-x))


def _silu(x):
    return x * _sigmoid(x)


def _log_sigmoid(x):
    return jnp.minimum(x, 0.0) - jnp.log1p(jnp.exp(-jnp.abs(x)))


def _dot(a, b):
    return jnp.dot(a, b, preferred_element_type=F32)


def _dot_nt(a, b):
    return lax.dot_general(a, b, (((1,), (1,)), ((), ())), preferred_element_type=F32)


def _dot_tn(a, b):
    return lax.dot_general(a, b, (((0,), (0,)), ((), ())), preferred_element_type=F32)


def _prenorm_gates(h, gpre, wg, bg):
    ms = jnp.mean(h * h, axis=-1, keepdims=True)
    hn = (h * lax.rsqrt(ms + EPS) * gpre).astype(BF16)
    gates = _dot(hn, wg) + bg
    return hn, gates


def _prenorm_kernel(h_ref, gpre_ref, wg_ref, bg_ref, hn_ref, gates_ref):
    hn, gates = _prenorm_gates(h_ref[...], gpre_ref[...], wg_ref[...], bg_ref[...])
    hn_ref[...] = hn
    gates_ref[...] = gates


def _prenorm(h, gpre, wg, bg):
    rows = h.shape[0]
    tm = _row_tile(rows)
    return pl.pallas_call(
        _prenorm_kernel,
        grid=(rows // tm,),
        in_specs=[
            pl.BlockSpec((tm, D_MODEL), lambda i: (i, 0)),
            pl.BlockSpec((1, D_MODEL), lambda i: (0, 0)),
            pl.BlockSpec((D_MODEL, GATE_PAD), lambda i: (0, 0)),
            pl.BlockSpec((1, GATE_PAD), lambda i: (0, 0)),
        ],
        out_specs=[
            pl.BlockSpec((tm, D_MODEL), lambda i: (i, 0)),
            pl.BlockSpec((tm, GATE_PAD), lambda i: (i, 0)),
        ],
        out_shape=[
            jax.ShapeDtypeStruct((rows, D_MODEL), BF16),
            jax.ShapeDtypeStruct((rows, GATE_PAD), F32),
        ],
        compiler_params=pltpu.CompilerParams(
            dimension_semantics=("arbitrary",), vmem_limit_bytes=VMEM_LIMIT),
        name="prenorm",
    )(h, gpre, wg, bg)


IN_TN = 2304
IN_SUB = 768


def _inproj_kernel(x_ref, w_ref, o_ref):
    x = x_ref[...]
    for c in range(IN_TN // IN_SUB):
        cols = slice(c * IN_SUB, (c + 1) * IN_SUB)
        o_ref[:, cols] = _dot(x, w_ref[:, cols]).astype(BF16)


def _inproj(hn, w):
    rows = hn.shape[0]
    tm = _row_tile(rows)
    return pl.pallas_call(
        _inproj_kernel,
        grid=(D_MAIN // IN_TN, rows // tm),
        in_specs=[
            pl.BlockSpec((tm, D_MODEL), lambda j, i: (i, 0)),
            pl.BlockSpec((D_MODEL, IN_TN), lambda j, i: (0, j)),
        ],
        out_specs=pl.BlockSpec((tm, IN_TN), lambda j, i: (i, j)),
        out_shape=jax.ShapeDtypeStruct((rows, D_MAIN), BF16),
        compiler_params=pltpu.CompilerParams(
            dimension_semantics=("arbitrary", "arbitrary"), vmem_limit_bytes=VMEM_LIMIT),
        name="inproj",
    )(hn, w)


def _mlstm_kernel(q_ref, k_ref, v_ref, o_ref, z_ref, g_ref, cwq_ref, cwk_ref, cbq_ref, cbk_ref,
                  gn_ref, out_ref, ct_ref, n_ref, m_ref, *, seq):
    lc = MLSTM_CHUNK
    nfull = seq // lc
    tail = seq - nfull * lc

    ct_ref[...] = jnp.zeros_like(ct_ref)
    n_ref[...] = jnp.zeros_like(n_ref)
    m_ref[...] = jnp.zeros_like(m_ref)

    ir = lax.broadcasted_iota(jnp.int32, (lc, lc), 0)
    ic = lax.broadcasted_iota(jnp.int32, (lc, lc), 1)
    row_id = lax.broadcasted_iota(jnp.int32, (lc, 1), 0)

    def conv_silu(x_ref, cw_ref, cb_ref, start, hist_start, hist_on):
        cur = x_ref[0, pl.ds(start, lc), :].astype(F32)
        hist = x_ref[0, pl.ds(hist_start, HALO), :].astype(F32) * hist_on
        win = jnp.concatenate([hist, cur], axis=0)
        cw = cw_ref[...]
        acc = cur * cw[CONV_WIDTH - 1:CONV_WIDTH, :] + cb_ref[...]
        for d in range(1, CONV_WIDTH):
            shifted = pltpu.roll(win, d, axis=0)[HALO:, :]
            acc = acc + shifted * cw[CONV_WIDTH - 1 - d:CONV_WIDTH - d, :]
        return _silu(acc)

    def chunk(start, hist_start, hist_on, row_lo):
        rows = pl.ds(start, lc)
        qf = conv_silu(q_ref, cwq_ref, cbq_ref, start, hist_start, hist_on) * (MLSTM_HEAD_DIM ** -0.5)
        kf = conv_silu(k_ref, cwk_ref, cbk_ref, start, hist_start, hist_on)
        qb = qf.astype(BF16)
        kb = kf.astype(BF16)
        vb = v_ref[0, rows, :]

        g = g_ref[0, 0, rows, :]
        li_col = g[:, 0:1]
        lf_col = _log_sigmoid(g[:, 1:2])
        if row_lo > 0:
            live = row_id >= row_lo
            li_col = jnp.where(live, li_col, NEG)
            lf_col = jnp.where(live, lf_col, 0.0)

        b_row = jnp.sum(jnp.where(ir <= ic, lf_col, 0.0), axis=0, keepdims=True)
        eye = ir == ic
        li_row = jnp.sum(jnp.where(eye, li_col, 0.0), axis=0, keepdims=True)
        b_col = jnp.sum(jnp.where(eye, b_row, 0.0), axis=1, keepdims=True)

        m_prev = m_ref[...]
        dmat = jnp.where(ic <= ir, b_col - b_row + li_row, NEG)
        inter = b_col + m_prev
        m_row = jnp.maximum(inter, jnp.max(dmat, axis=1, keepdims=True))
        w_intra = jnp.exp(dmat - m_row)
        w_inter = jnp.exp(inter - m_row)

        s = _dot_nt(qb, kb) * w_intra
        ct = ct_ref[...]
        n_prev = n_ref[...]
        num = _dot(s.astype(BF16), vb) + w_inter * _dot(qb, ct.astype(BF16))
        den = (jnp.sum(s, axis=1, keepdims=True)
               + w_inter * jnp.sum(qf * n_prev, axis=1, keepdims=True))
        h = num / jnp.maximum(jnp.abs(den), jnp.exp(-m_row))

        b_last = b_row[:, lc - 1:lc]
        g_col = b_last - b_col + li_col
        m_new = jnp.maximum(b_last + m_prev, jnp.max(g_col, axis=0, keepdims=True))
        wk = jnp.exp(g_col - m_new)
        decay = jnp.exp(b_last + m_prev - m_new)
        kw = kf * wk
        ct_ref[...] = decay * ct + _dot_tn(kw.astype(BF16), vb)
        n_ref[...] = decay * n_prev + jnp.sum(kw, axis=0, keepdims=True)
        m_ref[...] = m_new

        hm = h * _sigmoid(o_ref[0, rows, :].astype(F32))
        ms = jnp.mean(hm * hm, axis=1, keepdims=True)
        y = hm * lax.rsqrt(ms + EPS) * gn_ref[...] * _silu(z_ref[0, rows, :].astype(F32))
        return y.astype(BF16)

    def body(c, carry):
        start = pl.multiple_of(c * lc, lc)
        hist_start = pl.multiple_of(jnp.maximum(start - HALO, 0), HALO)
        hist_on = (c > 0).astype(F32)
        out_ref[0, pl.ds(start, lc), :] = chunk(start, hist_start, hist_on, 0)
        return carry

    lax.fori_loop(0, nfull, body, 0)
    if tail:
        start = seq - lc
        y = chunk(start, start - HALO, 1.0, lc - tail)
        out_ref[0, seq - tail:seq, :] = y[lc - tail:, :]


def _mlstm(proj3, gates_h, conv_w, conv_b, g_mlstm):
    bsz, seq, _ = proj3.shape
    assert seq >= MLSTM_CHUNK + HALO and (seq % MLSTM_CHUNK) % 16 == 0
    hd = MLSTM_HEAD_DIM

    def sec(s):
        return pl.BlockSpec((1, seq, hd), lambda b, h, s=s: (b, 0, s * MLSTM_HEADS + h))

    return pl.pallas_call(
        functools.partial(_mlstm_kernel, seq=seq),
        grid=(bsz, MLSTM_HEADS),
        in_specs=[
            sec(0), sec(1), sec(2), sec(3), sec(4),
            pl.BlockSpec((1, 1, seq, 2), lambda b, h: (b, h, 0, 0)),
            pl.BlockSpec((CONV_WIDTH, hd), lambda b, h: (0, h)),
            pl.BlockSpec((CONV_WIDTH, hd), lambda b, h: (0, MLSTM_HEADS + h)),
            pl.BlockSpec((1, hd), lambda b, h: (0, h)),
            pl.BlockSpec((1, hd), lambda b, h: (0, MLSTM_HEADS + h)),
            pl.BlockSpec((1, hd), lambda b, h: (0, h)),
        ],
        out_specs=pl.BlockSpec((1, seq, hd), lambda b, h: (b, 0, h)),
        out_shape=jax.ShapeDtypeStruct((bsz, seq, D_MLSTM), BF16),
        scratch_shapes=[
            pltpu.VMEM((hd, hd), F32),
            pltpu.VMEM((1, hd), F32),
            pltpu.VMEM((1, 1), F32),
        ],
        compiler_params=pltpu.CompilerParams(
            dimension_semantics=("arbitrary", "arbitrary"), vmem_limit_bytes=VMEM_LIMIT),
        name="mlstm",
    )(proj3, proj3, proj3, proj3, proj3, gates_h, conv_w, conv_w, conv_b, conv_b, g_mlstm)


def _attn_kernel(q_ref, k_ref, v_ref, z_ref, lamv_ref, gd_ref, out_ref, *, seq, lam_init):
    tq, tk = ATTN_TQ, ATTN_TK
    nq = seq // tq
    tail = seq - nq * tq

    lamv = lamv_ref[...]
    lam = (jnp.exp(jnp.sum(lamv[0:1] * lamv[1:2], axis=1, keepdims=True))
           - jnp.exp(jnp.sum(lamv[2:3] * lamv[3:4], axis=1, keepdims=True)) + lam_init)
    out_gain = gd_ref[0] * (1.0 - lam_init)

    def qblock(qstart, n, nfull_kv, diag_start, diag_lo):
        qf = q_ref[0, pl.ds(qstart, n), :].astype(F32) * (DIFF_HEAD_DIM ** -0.5)
        lane = lax.broadcasted_iota(jnp.int32, (n, 2 * DIFF_HEAD_DIM), 1)
        first = lane < DIFF_HEAD_DIM
        qs = jnp.concatenate([jnp.where(first, qf, 0.0), jnp.where(first, 0.0, qf)],
                             axis=0).astype(BF16)

        def kv_step(kstart, carry, mask):
            m, l, acc = carry
            s = _dot_nt(qs, k_ref[0, pl.ds(kstart, tk), :])
            if mask is not None:
                s = jnp.where(mask, s, NEG)
            m_new = jnp.maximum(m, jnp.max(s, axis=1, keepdims=True))
            alpha = jnp.exp(m - m_new)
            p = jnp.exp(s - m_new)
            l = alpha * l + jnp.sum(p, axis=1, keepdims=True)
            acc = alpha * acc + _dot(p.astype(BF16), v_ref[0, pl.ds(kstart, tk), :])
            return m_new, l, acc

        carry = (jnp.full((2 * n, 1), NEG, F32), jnp.zeros((2 * n, 1), F32),
                 jnp.zeros((2 * n, DIFF_V_DIM), F32))
        carry = lax.fori_loop(
            0, nfull_kv,
            lambda j, c: kv_step(pl.multiple_of(j * tk, tk), c, None), carry)

        r = lax.broadcasted_iota(jnp.int32, (2 * n, tk), 0)
        qpos = qstart + jnp.where(r >= n, r - n, r)
        kpos = diag_start + lax.broadcasted_iota(jnp.int32, (2 * n, tk), 1)
        mask = (kpos <= qpos) & (kpos >= diag_lo)
        _, l, acc = kv_step(diag_start, carry, mask)

        o = acc / l
        a = o[:n] - lam * o[n:]
        ms = jnp.mean(a * a, axis=1, keepdims=True)
        zf = z_ref[0, pl.ds(qstart, n), :].astype(F32)
        y = a * lax.rsqrt(ms + EPS) * out_gain * _silu(zf)
        out_ref[0, pl.ds(qstart, n), :] = y.astype(BF16)

    def body(i, carry):
        qstart = pl.multiple_of(i * tq, tq)
        jd = (i * tq) // tk
        dstart = pl.multiple_of(jd * tk, tk)
        qblock(qstart, tq, jd, dstart, dstart)
        return carry

    lax.fori_loop(0, nq, body, 0)
    if tail:
        qstart = nq * tq
        qblock(qstart, tail, qstart // tk, seq - tk, (qstart // tk) * tk)


def _attn(proj3, lamv, g_diff3, lam_init):
    bsz, seq, _ = proj3.shape
    assert seq >= ATTN_TK and (seq % ATTN_TQ) % 16 == 0
    w = DIFF_V_DIM
    base = 5 * D_MLSTM // w

    def sec(s):
        return pl.BlockSpec((1, seq, w), lambda b, h, s=s: (b, 0, base + s * DIFF_HEADS + h))

    return pl.pallas_call(
        functools.partial(_attn_kernel, seq=seq, lam_init=lam_init),
        grid=(bsz, DIFF_HEADS),
        in_specs=[
            sec(0), sec(1), sec(2), sec(3),
            pl.BlockSpec((4, DIFF_HEAD_DIM), lambda b, h: (0, 0)),
            pl.BlockSpec((1, 1, w), lambda b, h: (h, 0, 0)),
        ],
        out_specs=pl.BlockSpec((1, seq, w), lambda b, h: (b, 0, h)),
        out_shape=jax.ShapeDtypeStruct((bsz, seq, D_DIFF), BF16),
        compiler_params=pltpu.CompilerParams(
            dimension_semantics=("arbitrary", "arbitrary"), vmem_limit_bytes=VMEM_LIMIT),
        name="diffattn",
    )(proj3, proj3, proj3, proj3, lamv, g_diff3)


def _outproj_body(hm_ref, hd_ref, h_ref, wo_ref, gpost_ref):
    y = _dot(hm_ref[...], wo_ref[:D_MLSTM, :]) + _dot(hd_ref[...], wo_ref[D_MLSTM:, :])
    ms = jnp.mean(y * y, axis=-1, keepdims=True)
    return h_ref[...] + y * lax.rsqrt(ms + EPS) * gpost_ref[...]


def _outproj_next_kernel(hm_ref, hd_ref, h_ref, wo_ref, gpost_ref, gpre_ref, wg_ref, bg_ref,
                         hnew_ref, hn_ref, gates_ref):
    hnew = _outproj_body(hm_ref, hd_ref, h_ref, wo_ref, gpost_ref)
    hnew_ref[...] = hnew
    hn, gates = _prenorm_gates(hnew, gpre_ref[...], wg_ref[...], bg_ref[...])
    hn_ref[...] = hn
    gates_ref[...] = gates


def _outproj_last_kernel(hm_ref, hd_ref, h_ref, wo_ref, gpost_ref, hnew_ref):
    hnew_ref[...] = _outproj_body(hm_ref, hd_ref, h_ref, wo_ref, gpost_ref)


def _outproj(hm, hd, h, wo, gpost, nxt):
    rows = h.shape[0]
    tm = _row_tile(rows)
    row_spec = lambda width: pl.BlockSpec((tm, width), lambda i: (i, 0))
    const = lambda shape: pl.BlockSpec(shape, lambda i: (0, 0))
    in_specs = [row_spec(D_MLSTM), row_spec(D_DIFF), row_spec(D_MODEL),
                const((D_MLSTM + D_DIFF, D_MODEL)), const((1, D_MODEL))]
    params = pltpu.CompilerParams(dimension_semantics=("arbitrary",), vmem_limit_bytes=VMEM_LIMIT)
    if nxt is None:
        return pl.pallas_call(
            _outproj_last_kernel,
            grid=(rows // tm,),
            in_specs=in_specs,
            out_specs=row_spec(D_MODEL),
            out_shape=jax.ShapeDtypeStruct((rows, D_MODEL), F32),
            compiler_params=params,
            name="outproj_last",
        )(hm, hd, h, wo, gpost)
    gpre, wg, bg = nxt
    return pl.pallas_call(
        _outproj_next_kernel,
        grid=(rows // tm,),
        in_specs=in_specs + [const((1, D_MODEL)), const((D_MODEL, GATE_PAD)), const((1, GATE_PAD))],
        out_specs=[row_spec(D_MODEL), row_spec(D_MODEL), row_spec(GATE_PAD)],
        out_shape=[
            jax.ShapeDtypeStruct((rows, D_MODEL), F32),
            jax.ShapeDtypeStruct((rows, D_MODEL), BF16),
            jax.ShapeDtypeStruct((rows, GATE_PAD), F32),
        ],
        compiler_params=params,
        name="outproj",
    )(hm, hd, h, wo, gpost, gpre, wg, bg)


def kernel(x, meta_tokens, pre_norm_g, post_norm_g, w_in, b_gates, conv_w, conv_b, mlstm_norm_g,
           lambda_q1, lambda_k1, lambda_q2, lambda_k2, diff_norm_g, w_out):
    bsz, s_real, _ = x.shape
    depth = w_in.shape[0]
    seq = s_real + N_META
    rows = bsz * seq

    g0 = 5 * D_MLSTM
    w_main = jnp.concatenate([w_in[:, :, :g0], w_in[:, :, g0 + N_GATES:]], axis=-1).astype(BF16)
    w_gate = jnp.pad(w_in[:, :, g0:g0 + N_GATES],
                     ((0, 0), (0, 0), (0, GATE_PAD - N_GATES))).astype(BF16)
    b_gate = jnp.pad(b_gates, ((0, 0), (0, GATE_PAD - N_GATES)))[:, None, :]
    w_o = w_out.astype(BF16)
    lamv = jnp.stack([lambda_q1, lambda_k1, lambda_q2, lambda_k2], axis=1)
    g_diff = diff_norm_g.reshape(depth, DIFF_HEADS, 1, DIFF_V_DIM)

    meta = jnp.broadcast_to(meta_tokens[None].astype(x.dtype), (bsz, N_META, D_MODEL))
    h = jnp.concatenate([meta, x], axis=1).reshape(rows, D_MODEL)

    hn, gates = _prenorm(h, pre_norm_g[0][None], w_gate[0], b_gate[0])
    for layer in range(depth):
        lam_init = 0.8 - 0.6 * math.exp(-0.3 * layer)
        proj3 = _inproj(hn, w_main[layer]).reshape(bsz, seq, D_MAIN)
        gates_h = gates[:, :N_GATES].reshape(bsz, seq, 2, MLSTM_HEADS).transpose(0, 3, 1, 2)
        hm = _mlstm(proj3, gates_h, conv_w[layer], conv_b[layer][None], mlstm_norm_g[layer][None])
        hd = _attn(proj3, lamv[layer], g_diff[layer], lam_init)
        hm = hm.reshape(rows, D_MLSTM)
        hd = hd.reshape(rows, D_DIFF)
        if layer + 1 < depth:
            nxt = (pre_norm_g[layer + 1][None], w_gate[layer + 1], b_gate[layer + 1])
            h, hn, gates = _outproj(hm, hd, h, w_o[layer], post_norm_g[layer][None], nxt)
        else:
            h = _outproj(hm, hd, h, w_o[layer], post_norm_g[layer][None], None)
    return h.reshape(bsz, seq, D_MODEL)[:, N_META:]
```

```python
import functools
import math

import jax
import jax.numpy as jnp
from jax import lax
from jax.experimental import pallas as pl
from jax.experimental.pallas import tpu as pltpu

D_MODEL = 1024
N_META = 16
D_MLSTM = 1024
D_DIFF = 1024
MLSTM_HEADS = 4
MLSTM_HEAD_DIM = 256
CONV_WIDTH = 4
DIFF_HEADS = 8
DIFF_HEAD_DIM = 64
DIFF_V_DIM = 128
N_GATES = 2 * MLSTM_HEADS
D_MAIN = 5 * D_MLSTM + 4 * D_DIFF
EPS = 1e-6
NEG = -1e30

LANES = 128
GATE_PAD = LANES
MLSTM_CHUNK = 256
ATTN_T = 256
ATTN_RC = 64
HALO = 16
VMEM_LIMIT = 48 * 1024 * 1024

F32 = jnp.float32
BF16 = jnp.bfloat16


def _row_tile(rows, cap=768):
    best = None
    for t in range(16, cap + 1, 16):
        if rows % t == 0:
            best = t
    assert best is not None, rows
    return best


def _sigmoid(x):
    return 1.0 / (1.0 + jnp.exp(-x))


def _silu(x):
    return x * _sigmoid(x)


def _log_sigmoid(x):
    return jnp.minimum(x, 0.0) - jnp.log1p(jnp.exp(-jnp.abs(x)))


def _dot(a, b):
    return jnp.dot(a, b, preferred_element_type=F32)


def _dot_nt(a, b):
    return lax.dot_general(a, b, (((1,), (1,)), ((), ())), preferred_element_type=F32)


def _dot_tn(a, b):
    return lax.dot_general(a, b, (((0,), (0,)), ((), ())), preferred_element_type=F32)


def _prenorm_gates(h, gpre, wg, bg):
    ms = jnp.mean(h * h, axis=-1, keepdims=True)
    hn = (h * lax.rsqrt(ms + EPS) * gpre).astype(BF16)
    gates = _dot(hn, wg) + bg
    return hn, gates


def _prenorm_kernel(h_ref, gpre_ref, wg_ref, bg_ref, hn_ref, gates_ref):
    hn, gates = _prenorm_gates(h_ref[...], gpre_ref[...], wg_ref[...], bg_ref[...])
    hn_ref[...] = hn
    gates_ref[...] = gates


def _prenorm(h, gpre, wg, bg):
    rows = h.shape[0]
    tm = _row_tile(rows)
    return pl.pallas_call(
        _prenorm_kernel,
        grid=(rows // tm,),
        in_specs=[
            pl.BlockSpec((tm, D_MODEL), lambda i: (i, 0)),
            pl.BlockSpec((1, D_MODEL), lambda i: (0, 0)),
            pl.BlockSpec((D_MODEL, GATE_PAD), lambda i: (0, 0)),
            pl.BlockSpec((1, GATE_PAD), lambda i: (0, 0)),
        ],
        out_specs=[
            pl.BlockSpec((tm, D_MODEL), lambda i: (i, 0)),
            pl.BlockSpec((tm, GATE_PAD), lambda i: (i, 0)),
        ],
        out_shape=[
            jax.ShapeDtypeStruct((rows, D_MODEL), BF16),
            jax.ShapeDtypeStruct((rows, GATE_PAD), F32),
        ],
        compiler_params=pltpu.CompilerParams(
            dimension_semantics=("arbitrary",), vmem_limit_bytes=VMEM_LIMIT),
        name="prenorm",
    )(h, gpre, wg, bg)


IN_TN = 2304
IN_SUB = 768


def _inproj_kernel(x_ref, w_ref, o_ref):
    x = x_ref[...]
    for c in range(IN_TN // IN_SUB):
        cols = slice(c * IN_SUB, (c + 1) * IN_SUB)
        o_ref[:, cols] = _dot(x, w_ref[:, cols]).astype(BF16)


def _inproj(hn, w):
    rows = hn.shape[0]
    tm = _row_tile(rows)
    return pl.pallas_call(
        _inproj_kernel,
        grid=(D_MAIN // IN_TN, rows // tm),
        in_specs=[
            pl.BlockSpec((tm, D_MODEL), lambda j, i: (i, 0)),
            pl.BlockSpec((D_MODEL, IN_TN), lambda j, i: (0, j)),
        ],
        out_specs=pl.BlockSpec((tm, IN_TN), lambda j, i: (i, j)),
        out_shape=jax.ShapeDtypeStruct((rows, D_MAIN), BF16),
        compiler_params=pltpu.CompilerParams(
            dimension_semantics=("arbitrary", "arbitrary"), vmem_limit_bytes=VMEM_LIMIT),
        name="inproj",
    )(hn, w)


def _mlstm_kernel(q_ref, k_ref, v_ref, o_ref, z_ref, g_ref, cwq_ref, cwk_ref, cbq_ref, cbk_ref,
                  gn_ref, out_ref, ct_ref, n_ref, m_ref, *, seq):
    lc = MLSTM_CHUNK
    nfull = seq // lc
    tail = seq - nfull * lc

    ct_ref[...] = jnp.zeros_like(ct_ref)
    n_ref[...] = jnp.zeros_like(n_ref)
    m_ref[...] = jnp.zeros_like(m_ref)

    ir = lax.broadcasted_iota(jnp.int32, (lc, lc), 0)
    ic = lax.broadcasted_iota(jnp.int32, (lc, lc), 1)
    row_id = lax.broadcasted_iota(jnp.int32, (lc, 1), 0)

    def conv_silu(x_ref, cw_ref, cb_ref, start, hist_start, hist_on):
        cur = x_ref[0, pl.ds(start, lc), :].astype(F32)
        hist = x_ref[0, pl.ds(hist_start, HALO), :].astype(F32) * hist_on
        win = jnp.concatenate([hist, cur], axis=0)
        cw = cw_ref[...]
        acc = cur * cw[CONV_WIDTH - 1:CONV_WIDTH, :] + cb_ref[...]
        for d in range(1, CONV_WIDTH):
            shifted = pltpu.roll(win, d, axis=0)[HALO:, :]
            acc = acc + shifted * cw[CONV_WIDTH - 1 - d:CONV_WIDTH - d, :]
        return _silu(acc)

    def chunk(start, hist_start, hist_on, row_lo):
        rows = pl.ds(start, lc)
        qf = conv_silu(q_ref, cwq_ref, cbq_ref, start, hist_start, hist_on) * (MLSTM_HEAD_DIM ** -0.5)
        kf = conv_silu(k_ref, cwk_ref, cbk_ref, start, hist_start, hist_on)
        qb = qf.astype(BF16)
        kb = kf.astype(BF16)
        vb = v_ref[0, rows, :]

        g = g_ref[0, 0, rows, :]
        li_col = g[:, 0:1]
        lf_col = _log_sigmoid(g[:, 1:2])
        if row_lo > 0:
            live = row_id >= row_lo
            li_col = jnp.where(live, li_col, NEG)
            lf_col = jnp.where(live, lf_col, 0.0)

        b_row = jnp.sum(jnp.where(ir <= ic, lf_col, 0.0), axis=0, keepdims=True)
        eye = ir == ic
        li_row = jnp.sum(jnp.where(eye, li_col, 0.0), axis=0, keepdims=True)
        b_col = jnp.sum(jnp.where(eye, b_row, 0.0), axis=1, keepdims=True)

        m_prev = m_ref[...]
        dmat = jnp.where(ic <= ir, b_col - b_row + li_row, NEG)
        inter = b_col + m_prev
        m_row = jnp.maximum(inter, jnp.max(dmat, axis=1, keepdims=True))
        w_intra = jnp.exp(dmat - m_row)
        w_inter = jnp.exp(inter - m_row)

        s = _dot_nt(qb, kb) * w_intra
        ct = ct_ref[...]
        n_prev = n_ref[...]
        num = _dot(s.astype(BF16), vb) + w_inter * _dot(qb, ct.astype(BF16))
        den = (jnp.sum(s, axis=1, keepdims=True)
               + w_inter * jnp.sum(qf * n_prev, axis=1, keepdims=True))
        h = num / jnp.maximum(jnp.abs(den), jnp.exp(-m_row))

        b_last = b_row[:, lc - 1:lc]
        g_col = b_last - b_col + li_col
        m_new = jnp.maximum(b_last + m_prev, jnp.max(g_col, axis=0, keepdims=True))
        wk = jnp.exp(g_col - m_new)
        decay = jnp.exp(b_last + m_prev - m_new)
        kw = kf * wk
        ct_ref[...] = decay * ct + _dot_tn(kw.astype(BF16), vb)
        n_ref[...] = decay * n_prev + jnp.sum(kw, axis=0, keepdims=True)
        m_ref[...] = m_new

        hm = h * _sigmoid(o_ref[0, rows, :].astype(F32))
        ms = jnp.mean(hm * hm, axis=1, keepdims=True)
        y = hm * lax.rsqrt(ms + EPS) * gn_ref[...] * _silu(z_ref[0, rows, :].astype(F32))
        return y.astype(BF16)

    def body(c, carry):
        start = pl.multiple_of(c * lc, lc)
        hist_start = pl.multiple_of(jnp.maximum(start - HALO, 0), HALO)
        hist_on = (c > 0).astype(F32)
        out_ref[0, pl.ds(start, lc), :] = chunk(start, hist_start, hist_on, 0)
        return carry

    lax.fori_loop(0, nfull, body, 0)
    if tail:
        start = seq - lc
        y = chunk(start, start - HALO, 1.0, lc - tail)
        out_ref[0, seq - tail:seq, :] = y[lc - tail:, :]


def _mlstm(proj3, gates_h, conv_w, conv_b, g_mlstm):
    bsz, seq, _ = proj3.shape
    assert seq >= MLSTM_CHUNK + HALO and (seq % MLSTM_CHUNK) % 16 == 0
    hd = MLSTM_HEAD_DIM

    def sec(s):
        return pl.BlockSpec((1, seq, hd), lambda b, h, s=s: (b, 0, s * MLSTM_HEADS + h))

    return pl.pallas_call(
        functools.partial(_mlstm_kernel, seq=seq),
        grid=(bsz, MLSTM_HEADS),
        in_specs=[
            sec(0), sec(1), sec(2), sec(3), sec(4),
            pl.BlockSpec((1, 1, seq, 2), lambda b, h: (b, h, 0, 0)),
            pl.BlockSpec((CONV_WIDTH, hd), lambda b, h: (0, h)),
            pl.BlockSpec((CONV_WIDTH, hd), lambda b, h: (0, MLSTM_HEADS + h)),
            pl.BlockSpec((1, hd), lambda b, h: (0, h)),
            pl.BlockSpec((1, hd), lambda b, h: (0, MLSTM_HEADS + h)),
            pl.BlockSpec((1, hd), lambda b, h: (0, h)),
        ],
        out_specs=pl.BlockSpec((1, seq, hd), lambda b, h: (b, 0, h)),
        out_shape=jax.ShapeDtypeStruct((bsz, seq, D_MLSTM), BF16),
        scratch_shapes=[
            pltpu.VMEM((hd, hd), F32),
            pltpu.VMEM((1, hd), F32),
            pltpu.VMEM((1, 1), F32),
        ],
        compiler_params=pltpu.CompilerParams(
            dimension_semantics=("arbitrary", "arbitrary"), vmem_limit_bytes=VMEM_LIMIT),
        name="mlstm",
    )(proj3, proj3, proj3, proj3, proj3, gates_h, conv_w, conv_w, conv_b, conv_b, g_mlstm)


def _attn_kernel(q_ref, k_ref, v_ref, z_ref, lamv_ref, gd_ref, out_ref,
                 s_scr, p_scr, l_scr, bias_scr, *, seq, lam_init):
    t = ATTN_T
    nq = seq // t
    tail = seq - nq * t

    lamv = lamv_ref[...]
    lam = (jnp.exp(jnp.sum(lamv[0:1] * lamv[1:2], axis=1, keepdims=True))
           - jnp.exp(jnp.sum(lamv[2:3] * lamv[3:4], axis=1, keepdims=True)) + lam_init)
    out_gain = gd_ref[0] * (1.0 - lam_init)

    r = lax.broadcasted_iota(jnp.int32, (2 * t, t), 0)
    c = lax.broadcasted_iota(jnp.int32, (2 * t, t), 1)
    bias_scr[...] = jnp.where(c <= jnp.where(r >= t, r - t, r), 0.0, NEG)

    blocks = [(i * t, t, [(j * t, None) for j in range(i)] + [(i * t, "diag")]) for i in range(nq)]
    if tail:
        blocks.append((nq * t, tail, [(j * t, None) for j in range(nq)] + [(seq - t, "tail")]))

    def scores(bi):
        qstart, n, tiles = blocks[bi]
        qf = q_ref[0, qstart:qstart + n, :].astype(F32) * (DIFF_HEAD_DIM ** -0.5)
        first = lax.broadcasted_iota(jnp.int32, (n, 2 * DIFF_HEAD_DIM), 1) < DIFF_HEAD_DIM
        qs = jnp.concatenate([jnp.where(first, qf, 0.0), jnp.where(first, 0.0, qf)],
                             axis=0).astype(BF16)
        for ti, (kstart, kind) in enumerate(tiles):
            s = _dot_nt(qs, k_ref[0, kstart:kstart + t, :])
            if kind == "diag":
                s = s + bias_scr[...]
            elif kind == "tail":
                rr = lax.broadcasted_iota(jnp.int32, (2 * n, t), 0)
                qpos = qstart + jnp.where(rr >= n, rr - n, rr)
                kpos = kstart + lax.broadcasted_iota(jnp.int32, (2 * n, t), 1)
                s = jnp.where((kpos <= qpos) & (kpos >= qstart), s, NEG)
            s_scr[bi % 2, 0:2 * n, ti * t:(ti + 1) * t] = s

    def softmax(bi):
        _, n, tiles = blocks[bi]
        buf = bi % 2
        rc = min(ATTN_RC, 2 * n)
        ncol = len(tiles) * t // LANES
        for r0 in range(0, 2 * n, rc):
            rows = slice(r0, r0 + rc)
            mx = s_scr[buf, rows, 0:LANES]
            for ci in range(1, ncol):
                mx = jnp.maximum(mx, s_scr[buf, rows, ci * LANES:(ci + 1) * LANES])
            m = jnp.broadcast_to(jnp.max(mx, axis=1, keepdims=True), (rc, LANES))
            ls = jnp.zeros((rc, LANES), F32)
            for ci in range(ncol):
                cols = slice(ci * LANES, (ci + 1) * LANES)
                p = jnp.exp(s_scr[buf, rows, cols] - m)
                ls = ls + p
                p_scr[buf, rows, cols] = p.astype(BF16)
            l_scr[buf, rows, :] = jnp.sum(ls, axis=1, keepdims=True)

    def finish(bi):
        qstart, n, tiles = blocks[bi]
        buf = bi % 2
        nfull = len(tiles) - 1
        last_k = tiles[-1][0]
        acc = _dot(p_scr[buf, 0:2 * n, nfull * t:(nfull + 1) * t], v_ref[0, last_k:last_k + t, :])
        if nfull:
            acc = acc + _dot(p_scr[buf, 0:2 * n, 0:nfull * t], v_ref[0, 0:nfull * t, :])
        o = acc / l_scr[buf, 0:2 * n, :]
        a = o[:n] - lam * o[n:]
        ms = jnp.mean(a * a, axis=1, keepdims=True)
        zf = z_ref[0, qstart:qstart + n, :].astype(F32)
        y = a * lax.rsqrt(ms + EPS) * out_gain * _silu(zf)
        out_ref[0, qstart:qstart + n, :] = y.astype(BF16)

    scores(0)
    for bi in range(len(blocks)):
        if bi + 1 < len(blocks):
            scores(bi + 1)
        softmax(bi)
        finish(bi)


def _attn(proj3, lamv, g_diff3, lam_init):
    bsz, seq, _ = proj3.shape
    assert seq >= ATTN_T and (seq % ATTN_T) % 16 == 0
    w = DIFF_V_DIM
    width = -(-seq // ATTN_T) * ATTN_T
    base = 5 * D_MLSTM // w

    def sec(s):
        return pl.BlockSpec((1, seq, w), lambda b, h, s=s: (b, 0, base + s * DIFF_HEADS + h))

    return pl.pallas_call(
        functools.partial(_attn_kernel, seq=seq, lam_init=lam_init),
        grid=(bsz, DIFF_HEADS),
        in_specs=[
            sec(0), sec(1), sec(2), sec(3),
            pl.BlockSpec((4, DIFF_HEAD_DIM), lambda b, h: (0, 0)),
            pl.BlockSpec((1, 1, w), lambda b, h: (h, 0, 0)),
        ],
        out_specs=pl.BlockSpec((1, seq, w), lambda b, h: (b, 0, h)),
        out_shape=jax.ShapeDtypeStruct((bsz, seq, D_DIFF), BF16),
        scratch_shapes=[
            pltpu.VMEM((2, 2 * ATTN_T, width), F32),
            pltpu.VMEM((2, 2 * ATTN_T, width), BF16),
            pltpu.VMEM((2, 2 * ATTN_T, 1), F32),
            pltpu.VMEM((2 * ATTN_T, ATTN_T), F32),
        ],
        compiler_params=pltpu.CompilerParams(
            dimension_semantics=("arbitrary", "arbitrary"), vmem_limit_bytes=VMEM_LIMIT),
        name="diffattn",
    )(proj3, proj3, proj3, proj3, lamv, g_diff3)


def _outproj_body(hm_ref, hd_ref, h_ref, wo_ref, gpost_ref):
    y = _dot(hm_ref[...], wo_ref[:D_MLSTM, :]) + _dot(hd_ref[...], wo_ref[D_MLSTM:, :])
    ms = jnp.mean(y * y, axis=-1, keepdims=True)
    return h_ref[...] + y * lax.rsqrt(ms + EPS) * gpost_ref[...]


def _outproj_next_kernel(hm_ref, hd_ref, h_ref, wo_ref, gpost_ref, gpre_ref, wg_ref, bg_ref,
                         hnew_ref, hn_ref, gates_ref):
    hnew = _outproj_body(hm_ref, hd_ref, h_ref, wo_ref, gpost_ref)
    hnew_ref[...] = hnew
    hn, gates = _prenorm_gates(hnew, gpre_ref[...], wg_ref[...], bg_ref[...])
    hn_ref[...] = hn
    gates_ref[...] = gates


def _outproj_last_kernel(hm_ref, hd_ref, h_ref, wo_ref, gpost_ref, hnew_ref):
    hnew_ref[...] = _outproj_body(hm_ref, hd_ref, h_ref, wo_ref, gpost_ref)


def _outproj(hm, hd, h, wo, gpost, nxt):
    rows = h.shape[0]
    tm = _row_tile(rows)
    row_spec = lambda width: pl.BlockSpec((tm, width), lambda i: (i, 0))
    const = lambda shape: pl.BlockSpec(shape, lambda i: (0, 0))
    in_specs = [row_spec(D_MLSTM), row_spec(D_DIFF), row_spec(D_MODEL),
                const((D_MLSTM + D_DIFF, D_MODEL)), const((1, D_MODEL))]
    params = pltpu.CompilerParams(dimension_semantics=("arbitrary",), vmem_limit_bytes=VMEM_LIMIT)
    if nxt is None:
        return pl.pallas_call(
            _outproj_last_kernel,
            grid=(rows // tm,),
            in_specs=in_specs,
            out_specs=row_spec(D_MODEL),
            out_shape=jax.ShapeDtypeStruct((rows, D_MODEL), F32),
            compiler_params=params,
            name="outproj_last",
        )(hm, hd, h, wo, gpost)
    gpre, wg, bg = nxt
    return pl.pallas_call(
        _outproj_next_kernel,
        grid=(rows // tm,),
        in_specs=in_specs + [const((1, D_MODEL)), const((D_MODEL, GATE_PAD)), const((1, GATE_PAD))],
        out_specs=[row_spec(D_MODEL), row_spec(D_MODEL), row_spec(GATE_PAD)],
        out_shape=[
            jax.ShapeDtypeStruct((rows, D_MODEL), F32),
            jax.ShapeDtypeStruct((rows, D_MODEL), BF16),
            jax.ShapeDtypeStruct((rows, GATE_PAD), F32),
        ],
        compiler_params=params,
        name="outproj",
    )(hm, hd, h, wo, gpost, gpre, wg, bg)


def kernel(x, meta_tokens, pre_norm_g, post_norm_g, w_in, b_gates, conv_w, conv_b, mlstm_norm_g,
           lambda_q1, lambda_k1, lambda_q2, lambda_k2, diff_norm_g, w_out):
    bsz, s_real, _ = x.shape
    depth = w_in.shape[0]
    seq = s_real + N_META
    rows = bsz * seq

    g0 = 5 * D_MLSTM
    w_main = jnp.concatenate([w_in[:, :, :g0], w_in[:, :, g0 + N_GATES:]], axis=-1).astype(BF16)
    w_gate = jnp.pad(w_in[:, :, g0:g0 + N_GATES],
                     ((0, 0), (0, 0), (0, GATE_PAD - N_GATES))).astype(BF16)
    b_gate = jnp.pad(b_gates, ((0, 0), (0, GATE_PAD - N_GATES)))[:, None, :]
    w_o = w_out.astype(BF16)
    lamv = jnp.stack([lambda_q1, lambda_k1, lambda_q2, lambda_k2], axis=1)
    g_diff = diff_norm_g.reshape(depth, DIFF_HEADS, 1, DIFF_V_DIM)

    meta = jnp.broadcast_to(meta_tokens[None].astype(x.dtype), (bsz, N_META, D_MODEL))
    h = jnp.concatenate([meta, x], axis=1).reshape(rows, D_MODEL)

    hn, gates = _prenorm(h, pre_norm_g[0][None], w_gate[0], b_gate[0])
    for layer in range(depth):
        lam_init = 0.8 - 0.6 * math.exp(-0.3 * layer)
        proj3 = _inproj(hn, w_main[layer]).reshape(bsz, seq, D_MAIN)
        gates_h = gates[:, :N_GATES].reshape(bsz, seq, 2, MLSTM_HEADS).transpose(0, 3, 1, 2)
        hm = _mlstm(proj3, gates_h, conv_w[layer], conv_b[layer][None], mlstm_norm_g[layer][None])
        hd = _attn(proj3, lamv[layer], g_diff[layer], lam_init)
        hm = hm.reshape(rows, D_MLSTM)
        hd = hd.reshape(rows, D_DIFF)
        if layer + 1 < depth:
            nxt = (pre_norm_g[layer + 1][None], w_gate[layer + 1], b_gate[layer + 1])
            h, hn, gates = _outproj(hm, hd, h, w_o[layer], post_norm_g[layer][None], nxt)
        else:
            h = _outproj(hm, hd, h, w_o[layer], post_norm_g[layer][None], None)
    return h.reshape(bsz, seq, D_MODEL)[:, N_META:]
```

```python
import functools
import math

import jax
import jax.numpy as jnp
from jax import lax
from jax.experimental import pallas as pl
from jax.experimental.pallas import tpu as pltpu

D_MODEL = 1024
N_META = 16
D_MLSTM = 1024
D_DIFF = 1024
MLSTM_HEADS = 4
MLSTM_HEAD_DIM = 256
CONV_WIDTH = 4
DIFF_HEADS = 8
DIFF_HEAD_DIM = 64
DIFF_V_DIM = 128
N_GATES = 2 * MLSTM_HEADS
SECTION = 1024
D_MAIN = 9 * SECTION
EPS = 1e-6
NEG = -1e30
LOG2E = math.log2(math.e)

LANES = 128
SUBLANES = 8
BF16_ROWS = 16
GATE_PAD = LANES
MLSTM_CHUNK = 256
MLSTM_TAIL = 128
ATTN_T = 256
ATTN_RC = 64
IN_SUB = 256
CONV_ROWS = 64
VMEM_LIMIT = 48 * 1024 * 1024

F32 = jnp.float32
BF16 = jnp.bfloat16


def _row_tile(rows, cap=768, mult=BF16_ROWS):
    best = None
    for t in range(mult, cap + 1, mult):
        if rows % t == 0:
            best = t
    assert best is not None, rows
    return best


def _sigmoid(x):
    return 0.5 * jnp.tanh(0.5 * x) + 0.5


def _silu(x):
    return x * _sigmoid(x)


def _log_sigmoid(x):
    return jnp.minimum(x, 0.0) - jnp.log1p(jnp.exp(-jnp.abs(x)))


def _dot(a, b):
    return jnp.dot(a, b, preferred_element_type=F32)


def _dot_nt(a, b):
    return lax.dot_general(a, b, (((1,), (1,)), ((), ())), preferred_element_type=F32)


def _dot_tn(a, b):
    return lax.dot_general(a, b, (((0,), (0,)), ((), ())), preferred_element_type=F32)


def _prenorm_gates(h, gpre, wg, bg):
    ms = jnp.mean(h * h, axis=-1, keepdims=True)
    hn = (h * lax.rsqrt(ms + EPS) * gpre).astype(BF16)
    pre = _dot(hn, wg) + bg
    lane = lax.broadcasted_iota(jnp.int32, pre.shape, 1)
    gates = jnp.where(lane < MLSTM_HEADS, pre, _log_sigmoid(pre))
    return hn, gates


def _prenorm_kernel(h_ref, gpre_ref, wg_ref, bg_ref, hn_ref, gates_ref):
    hn, gates = _prenorm_gates(h_ref[...], gpre_ref[...], wg_ref[...], bg_ref[...])
    hn_ref[...] = hn
    gates_ref[...] = gates


def _prenorm(h, gpre, wg, bg):
    rows = h.shape[0]
    tm = _row_tile(rows)
    return pl.pallas_call(
        _prenorm_kernel,
        grid=(rows // tm,),
        in_specs=[
            pl.BlockSpec((tm, D_MODEL), lambda i: (i, 0)),
            pl.BlockSpec((1, D_MODEL), lambda i: (0, 0)),
            pl.BlockSpec((D_MODEL, GATE_PAD), lambda i: (0, 0)),
            pl.BlockSpec((1, GATE_PAD), lambda i: (0, 0)),
        ],
        out_specs=[
            pl.BlockSpec((tm, D_MODEL), lambda i: (i, 0)),
            pl.BlockSpec((tm, GATE_PAD), lambda i: (i, 0)),
        ],
        out_shape=[
            jax.ShapeDtypeStruct((rows, D_MODEL), BF16),
            jax.ShapeDtypeStruct((rows, GATE_PAD), F32),
        ],
        compiler_params=pltpu.CompilerParams(
            dimension_semantics=("arbitrary",), vmem_limit_bytes=VMEM_LIMIT),
        name="prenorm",
    )(h, gpre, wg, bg)


SEC_QM, SEC_KM, SEC_VM, SEC_OM, SEC_ZM, SEC_QD, SEC_KD, SEC_VD, SEC_ZD = range(9)
Q_D_SCALE = DIFF_HEAD_DIM ** -0.5 * LOG2E


def _inproj_kernel(x_ref, w_ref, cw_ref, cb_ref, o_ref, acc_scr, *, seq):
    j = pl.program_id(0)
    x = x_ref[0]
    nsub = SECTION // IN_SUB

    def elementwise(f):
        for c in range(nsub):
            cols = slice(c * IN_SUB, (c + 1) * IN_SUB)
            o_ref[0, :, cols] = f(_dot(x, w_ref[:, cols])).astype(BF16)

    @pl.when((j == SEC_VM) | (j == SEC_KD) | (j == SEC_VD))
    def _():
        elementwise(lambda y: y)

    @pl.when(j == SEC_OM)
    def _():
        elementwise(_sigmoid)

    @pl.when((j == SEC_ZM) | (j == SEC_ZD))
    def _():
        elementwise(_silu)

    @pl.when(j == SEC_QD)
    def _():
        elementwise(lambda y: y * Q_D_SCALE)

    @pl.when(j <= SEC_KM)
    def _():
        qscale = jnp.where(j == SEC_QM, MLSTM_HEAD_DIM ** -0.5, 1.0).astype(F32)
        rc = _row_tile(seq, cap=CONV_ROWS)
        pad = SUBLANES

        def col_slice(c):
            return slice(c * IN_SUB, (c + 1) * IN_SUB)

        def matmul(c):
            acc_scr[c % 2, pad:pad + seq, :] = _dot(x, w_ref[:, col_slice(c)])

        def conv(c):
            cw = 0.5 * cw_ref[:, col_slice(c)]
            cb = 0.5 * cb_ref[:, col_slice(c)]
            for r0 in range(0, seq, rc):
                hy = cb
                for d in range(CONV_WIDTH):
                    lo = pad + r0 - d
                    hy = hy + acc_scr[c % 2, lo:lo + rc, :] * cw[CONV_WIDTH - 1 - d:CONV_WIDTH - d, :]
                y = (hy * jnp.tanh(hy) + hy) * qscale
                o_ref[0, r0:r0 + rc, col_slice(c)] = y.astype(BF16)

        acc_scr[:, 0:pad, :] = jnp.zeros((2, pad, IN_SUB), F32)
        matmul(0)
        for c in range(nsub):
            if c + 1 < nsub:
                matmul(c + 1)
            conv(c)


def _inproj(hn3, w, conv_w, conv_b):
    bsz, seq, _ = hn3.shape
    nsec = D_MAIN // SECTION
    return pl.pallas_call(
        functools.partial(_inproj_kernel, seq=seq),
        grid=(nsec, bsz),
        in_specs=[
            pl.BlockSpec((1, seq, D_MODEL), lambda j, b: (b, 0, 0)),
            pl.BlockSpec((D_MODEL, SECTION), lambda j, b: (0, j)),
            pl.BlockSpec((CONV_WIDTH, SECTION), lambda j, b: (0, jnp.minimum(j, SEC_KM))),
            pl.BlockSpec((1, SECTION), lambda j, b: (0, jnp.minimum(j, SEC_KM))),
        ],
        out_specs=pl.BlockSpec((1, seq, SECTION), lambda j, b: (b, 0, j)),
        out_shape=jax.ShapeDtypeStruct((bsz, seq, D_MAIN), BF16),
        scratch_shapes=[pltpu.VMEM((2, SUBLANES + seq, IN_SUB), F32)],
        compiler_params=pltpu.CompilerParams(
            dimension_semantics=("arbitrary", "arbitrary"), vmem_limit_bytes=VMEM_LIMIT),
        name="inproj",
    )(hn3, w, conv_w, conv_b)


def _mlstm_kernel(q_ref, k_ref, v_ref, o_ref, z_ref, g_ref, gr_ref, grt_ref, gn_ref, out_ref,
                  w_scr, winter_scr, floor_scr, wk_scr, dec_scr, vx_scr, ctx_scr, *, seq):
    hd = MLSTM_HEAD_DIM
    nfull = seq // MLSTM_CHUNK
    tail = seq - nfull * MLSTM_CHUNK
    chunks = [(c * MLSTM_CHUNK, MLSTM_CHUNK, 0, gr_ref[0, 0, :, c * MLSTM_CHUNK:(c + 1) * MLSTM_CHUNK])
              for c in range(nfull)]
    if tail:
        chunks.append((seq - MLSTM_TAIL, MLSTM_TAIL, MLSTM_TAIL - tail, grt_ref[0, 0]))

    vx_scr[:, 0:hd] = v_ref[0]
    vx_scr[:, hd:] = jnp.ones((seq, LANES), BF16)
    ctx_scr[...] = jnp.zeros_like(ctx_scr)

    m_prev = jnp.zeros((1, 1), F32)
    for ci, (start, lc, row_lo, gr) in enumerate(chunks):
        g = g_ref[0, 0, start:start + lc, :]
        li_col, lf_col = g[:, 0:1], g[:, 1:2]
        li_row, lf_row = gr[0:1, :], gr[1:2, :]
        if row_lo > 0:
            live_c = lax.broadcasted_iota(jnp.int32, (lc, 1), 0) >= row_lo
            live_r = lax.broadcasted_iota(jnp.int32, (1, lc), 1) >= row_lo
            li_col = jnp.where(live_c, li_col, NEG)
            lf_col = jnp.where(live_c, lf_col, 0.0)
            li_row = jnp.where(live_r, li_row, NEG)
            lf_row = jnp.where(live_r, lf_row, 0.0)
        ir = lax.broadcasted_iota(jnp.int32, (lc, lc), 0)
        ic = lax.broadcasted_iota(jnp.int32, (lc, lc), 1)
        causal = ic <= ir
        b_col = jnp.sum(jnp.where(causal, lf_row, 0.0), axis=1, keepdims=True)
        b_row = jnp.sum(jnp.where(ir <= ic, lf_col, 0.0), axis=0, keepdims=True)
        c_row = li_row - b_row
        c_col = li_col - b_col
        cmax = jnp.max(jnp.where(causal, c_row, NEG), axis=1, keepdims=True)
        m_col = jnp.maximum(m_prev, cmax)
        w_scr[ci, 0:lc, 0:lc] = jnp.exp(jnp.where(causal, c_row - m_col, NEG))
        m_rep = jnp.broadcast_to(m_col, (lc, LANES))
        winter_scr[ci, 0:lc, :] = jnp.exp(m_prev - m_rep)
        floor_scr[ci, 0:lc, :] = jnp.exp(-jnp.broadcast_to(b_col, (lc, LANES)) - m_rep)
        m_last = m_col[lc - 1:lc, :]
        wk_scr[ci, 0:lc, :] = jnp.broadcast_to(jnp.exp(c_col - m_last), (lc, LANES))
        dec_scr[ci] = jnp.broadcast_to(jnp.exp(m_prev - m_last), (1, LANES))
        m_prev = b_row[:, lc - 1:lc] + m_last

    for ci, (start, lc, row_lo, _) in enumerate(chunks):
        rows = slice(start, start + lc)
        qb = q_ref[0, rows, :]
        kb = k_ref[0, rows, :]
        vxb = vx_scr[rows, :]
        s = _dot_nt(qb, kb) * w_scr[ci, 0:lc, 0:lc]
        ctx = ctx_scr[...]
        w_inter = winter_scr[ci, 0:lc, :]
        ext = (_dot(s.astype(BF16), vxb)
               + jnp.concatenate([w_inter] * 3, axis=1) * _dot(qb, ctx.astype(BF16)))
        den = ext[:, hd:]
        rinv = 1.0 / jnp.maximum(jnp.abs(den), floor_scr[ci, 0:lc, :])
        h = ext[:, 0:hd] * jnp.concatenate([rinv] * 2, axis=1)

        wk = wk_scr[ci, 0:lc, :]
        kw = (kb.astype(F32) * jnp.concatenate([wk] * 2, axis=1)).astype(BF16)
        ctx_scr[...] = jnp.concatenate([dec_scr[ci]] * 3, axis=1) * ctx + _dot_tn(kw, vxb)

        hm = h * o_ref[0, rows, :].astype(F32)
        ms = jnp.mean(hm * hm, axis=1, keepdims=True)
        y = (hm * lax.rsqrt(ms + EPS) * gn_ref[...] * z_ref[0, rows, :].astype(F32)).astype(BF16)
        out_ref[0, start + row_lo:start + lc, :] = y[row_lo:, :]


def _mlstm(proj3, gates, g_mlstm):
    bsz, seq, _ = proj3.shape
    nfull = seq // MLSTM_CHUNK
    tail = seq % MLSTM_CHUNK
    assert nfull >= 1 and tail % BF16_ROWS == 0 and tail <= MLSTM_TAIL
    hd = MLSTM_HEAD_DIM
    nchunk = nfull + (1 if tail else 0)

    g4 = gates[:, :N_GATES].reshape(bsz, seq, 2, MLSTM_HEADS)
    g_col = g4.transpose(0, 3, 1, 2)
    g_row = g4.transpose(0, 3, 2, 1)
    g_row_tail = g_row[..., seq - MLSTM_TAIL:]

    def sec(s):
        return pl.BlockSpec((1, seq, hd), lambda b, h, s=s: (b, 0, s * MLSTM_HEADS + h))

    return pl.pallas_call(
        functools.partial(_mlstm_kernel, seq=seq),
        grid=(bsz, MLSTM_HEADS),
        in_specs=[
            sec(SEC_QM), sec(SEC_KM), sec(SEC_VM), sec(SEC_OM), sec(SEC_ZM),
            pl.BlockSpec((1, 1, seq, 2), lambda b, h: (b, h, 0, 0)),
            pl.BlockSpec((1, 1, 2, seq), lambda b, h: (b, h, 0, 0)),
            pl.BlockSpec((1, 1, 2, MLSTM_TAIL), lambda b, h: (b, h, 0, 0)),
            pl.BlockSpec((1, hd), lambda b, h: (0, h)),
        ],
        out_specs=pl.BlockSpec((1, seq, hd), lambda b, h: (b, 0, h)),
        out_shape=jax.ShapeDtypeStruct((bsz, seq, D_MLSTM), BF16),
        scratch_shapes=[
            pltpu.VMEM((nchunk, MLSTM_CHUNK, MLSTM_CHUNK), F32),
            pltpu.VMEM((nchunk, MLSTM_CHUNK, LANES), F32),
            pltpu.VMEM((nchunk, MLSTM_CHUNK, LANES), F32),
            pltpu.VMEM((nchunk, MLSTM_CHUNK, LANES), F32),
            pltpu.VMEM((nchunk, 1, LANES), F32),
            pltpu.VMEM((seq, hd + LANES), BF16),
            pltpu.VMEM((hd, hd + LANES), F32),
        ],
        compiler_params=pltpu.CompilerParams(
            dimension_semantics=("arbitrary", "arbitrary"), vmem_limit_bytes=VMEM_LIMIT),
        name="mlstm",
    )(proj3, proj3, proj3, proj3, proj3, g_col, g_row, g_row_tail, g_mlstm)


def _attn_kernel(q_ref, k_ref, v_ref, z_ref, lamv_ref, gd_ref, out_ref,
                 s_scr, p_scr, bias_scr, vx_scr, *, seq, lam_init):
    t = ATTN_T
    nq = seq // t
    tail = seq - nq * t

    lamv = lamv_ref[...]
    lam = (jnp.exp(jnp.sum(lamv[0:1] * lamv[1:2], axis=1, keepdims=True))
           - jnp.exp(jnp.sum(lamv[2:3] * lamv[3:4], axis=1, keepdims=True)) + lam_init)
    out_gain = gd_ref[0] * (1.0 - lam_init)

    vx_scr[:, 0:DIFF_V_DIM] = v_ref[0]
    vx_scr[:, DIFF_V_DIM:] = jnp.ones((seq, DIFF_V_DIM), BF16)

    r = lax.broadcasted_iota(jnp.int32, (2 * t, t), 0)
    c = lax.broadcasted_iota(jnp.int32, (2 * t, t), 1)
    bias_scr[...] = jnp.where(c <= jnp.where(r >= t, r - t, r), 0.0, NEG)

    blocks = [(i * t, t, [(j * t, None) for j in range(i)] + [(i * t, "diag")]) for i in range(nq)]
    if tail:
        blocks.append((nq * t, tail, [(j * t, None) for j in range(nq)] + [(seq - t, "tail")]))

    def scores(bi):
        qstart, n, tiles = blocks[bi]
        qf = q_ref[0, qstart:qstart + n, :].astype(F32)
        first = lax.broadcasted_iota(jnp.int32, (n, 2 * DIFF_HEAD_DIM), 1) < DIFF_HEAD_DIM
        qs = jnp.concatenate([jnp.where(first, qf, 0.0), jnp.where(first, 0.0, qf)],
                             axis=0).astype(BF16)
        for ti, (kstart, kind) in enumerate(tiles):
            s = _dot_nt(qs, k_ref[0, kstart:kstart + t, :])
            if kind == "diag":
                s = s + bias_scr[...]
            elif kind == "tail":
                rr = lax.broadcasted_iota(jnp.int32, (2 * n, t), 0)
                qpos = qstart + jnp.where(rr >= n, rr - n, rr)
                kpos = kstart + lax.broadcasted_iota(jnp.int32, (2 * n, t), 1)
                s = jnp.where((kpos <= qpos) & (kpos >= qstart), s, NEG)
            s_scr[bi % 2, 0:2 * n, ti * t:(ti + 1) * t] = s

    def softmax(bi):
        _, n, tiles = blocks[bi]
        buf = bi % 2
        rc = min(ATTN_RC, 2 * n)
        ncol = len(tiles) * t // LANES
        for r0 in range(0, 2 * n, rc):
            rows = slice(r0, r0 + rc)
            mx = s_scr[buf, rows, 0:LANES]
            for ci in range(1, ncol):
                mx = jnp.maximum(mx, s_scr[buf, rows, ci * LANES:(ci + 1) * LANES])
            m = jnp.broadcast_to(jnp.max(mx, axis=1, keepdims=True), (rc, LANES))
            for ci in range(ncol):
                cols = slice(ci * LANES, (ci + 1) * LANES)
                p_scr[buf, rows, cols] = jnp.exp2(s_scr[buf, rows, cols] - m).astype(BF16)

    def finish(bi):
        qstart, n, tiles = blocks[bi]
        buf = bi % 2
        nfull = len(tiles) - 1
        last_k = tiles[-1][0]
        acc = _dot(p_scr[buf, 0:2 * n, nfull * t:(nfull + 1) * t], vx_scr[last_k:last_k + t, :])
        if nfull:
            acc = acc + _dot(p_scr[buf, 0:2 * n, 0:nfull * t], vx_scr[0:nfull * t, :])
        o = acc[:, 0:DIFF_V_DIM] / acc[:, DIFF_V_DIM:]
        a = o[:n] - lam * o[n:]
        ms = jnp.mean(a * a, axis=1, keepdims=True)
        y = a * lax.rsqrt(ms + EPS) * out_gain * z_ref[0, qstart:qstart + n, :].astype(F32)
        out_ref[0, qstart:qstart + n, :] = y.astype(BF16)

    for bi in range(len(blocks)):
        scores(bi)
        softmax(bi)
        finish(bi)


def _attn(proj3, lamv, g_diff3, lam_init):
    bsz, seq, _ = proj3.shape
    assert seq >= ATTN_T and (seq % ATTN_T) % BF16_ROWS == 0
    w = DIFF_V_DIM
    width = -(-seq // ATTN_T) * ATTN_T
    per_sec = SECTION // w

    def sec(s):
        return pl.BlockSpec((1, seq, w), lambda b, h, s=s: (b, 0, s * per_sec + h))

    return pl.pallas_call(
        functools.partial(_attn_kernel, seq=seq, lam_init=lam_init),
        grid=(bsz, DIFF_HEADS),
        in_specs=[
            sec(SEC_QD), sec(SEC_KD), sec(SEC_VD), sec(SEC_ZD),
            pl.BlockSpec((4, DIFF_HEAD_DIM), lambda b, h: (0, 0)),
            pl.BlockSpec((1, 1, w), lambda b, h: (h, 0, 0)),
        ],
        out_specs=pl.BlockSpec((1, seq, w), lambda b, h: (b, 0, h)),
        out_shape=jax.ShapeDtypeStruct((bsz, seq, D_DIFF), BF16),
        scratch_shapes=[
            pltpu.VMEM((2, 2 * ATTN_T, width), F32),
            pltpu.VMEM((2, 2 * ATTN_T, width), BF16),
            pltpu.VMEM((2 * ATTN_T, ATTN_T), F32),
            pltpu.VMEM((seq, 2 * w), BF16),
        ],
        compiler_params=pltpu.CompilerParams(
            dimension_semantics=("arbitrary", "arbitrary"), vmem_limit_bytes=VMEM_LIMIT),
        name="diffattn",
    )(proj3, proj3, proj3, proj3, lamv, g_diff3)


def _outproj_body(hm_ref, hd_ref, h_ref, wo_ref, gpost_ref):
    y = _dot(hm_ref[...], wo_ref[:D_MLSTM, :]) + _dot(hd_ref[...], wo_ref[D_MLSTM:, :])
    ms = jnp.mean(y * y, axis=-1, keepdims=True)
    return h_ref[...] + y * lax.rsqrt(ms + EPS) * gpost_ref[...]


def _outproj_next_kernel(hm_ref, hd_ref, h_ref, wo_ref, gpost_ref, gpre_ref, wg_ref, bg_ref,
                         hnew_ref, hn_ref, gates_ref):
    hnew = _outproj_body(hm_ref, hd_ref, h_ref, wo_ref, gpost_ref)
    hnew_ref[...] = hnew
    hn, gates = _prenorm_gates(hnew, gpre_ref[...], wg_ref[...], bg_ref[...])
    hn_ref[...] = hn
    gates_ref[...] = gates


def _outproj_last_kernel(hm_ref, hd_ref, h_ref, wo_ref, gpost_ref, hnew_ref):
    hnew_ref[...] = _outproj_body(hm_ref, hd_ref, h_ref, wo_ref, gpost_ref)


def _outproj(hm, hd, h, wo, gpost, nxt):
    rows = h.shape[0]
    tm = _row_tile(rows)
    row_spec = lambda width: pl.BlockSpec((tm, width), lambda i: (i, 0))
    const = lambda shape: pl.BlockSpec(shape, lambda i: (0, 0))
    in_specs = [row_spec(D_MLSTM), row_spec(D_DIFF), row_spec(D_MODEL),
                const((D_MLSTM + D_DIFF, D_MODEL)), const((1, D_MODEL))]
    params = pltpu.CompilerParams(dimension_semantics=("arbitrary",), vmem_limit_bytes=VMEM_LIMIT)
    if nxt is None:
        return pl.pallas_call(
            _outproj_last_kernel,
            grid=(rows // tm,),
            in_specs=in_specs,
            out_specs=row_spec(D_MODEL),
            out_shape=jax.ShapeDtypeStruct((rows, D_MODEL), F32),
            compiler_params=params,
            name="outproj_last",
        )(hm, hd, h, wo, gpost)
    gpre, wg, bg = nxt
    return pl.pallas_call(
        _outproj_next_kernel,
        grid=(rows // tm,),
        in_specs=in_specs + [const((1, D_MODEL)), const((D_MODEL, GATE_PAD)), const((1, GATE_PAD))],
        out_specs=[row_spec(D_MODEL), row_spec(D_MODEL), row_spec(GATE_PAD)],
        out_shape=[
            jax.ShapeDtypeStruct((rows, D_MODEL), F32),
            jax.ShapeDtypeStruct((rows, D_MODEL), BF16),
            jax.ShapeDtypeStruct((rows, GATE_PAD), F32),
        ],
        compiler_params=params,
        name="outproj",
    )(hm, hd, h, wo, gpost, gpre, wg, bg)


def kernel(x, meta_tokens, pre_norm_g, post_norm_g, w_in, b_gates, conv_w, conv_b, mlstm_norm_g,
           lambda_q1, lambda_k1, lambda_q2, lambda_k2, diff_norm_g, w_out):
    bsz, s_real, _ = x.shape
    depth = w_in.shape[0]
    seq = s_real + N_META
    rows = bsz * seq

    g0 = 5 * D_MLSTM
    w_main = jnp.concatenate([w_in[:, :, :g0], w_in[:, :, g0 + N_GATES:]], axis=-1).astype(BF16)
    w_gate = jnp.pad(w_in[:, :, g0:g0 + N_GATES],
                     ((0, 0), (0, 0), (0, GATE_PAD - N_GATES))).astype(BF16)
    b_gate = jnp.pad(b_gates, ((0, 0), (0, GATE_PAD - N_GATES)))[:, None, :]
    w_o = w_out.astype(BF16)
    lamv = jnp.stack([lambda_q1, lambda_k1, lambda_q2, lambda_k2], axis=1)
    g_diff = diff_norm_g.reshape(depth, DIFF_HEADS, 1, DIFF_V_DIM)

    meta = jnp.broadcast_to(meta_tokens[None].astype(x.dtype), (bsz, N_META, D_MODEL))
    h = jnp.concatenate([meta, x], axis=1).reshape(rows, D_MODEL)

    hn, gates = _prenorm(h, pre_norm_g[0][None], w_gate[0], b_gate[0])
    for layer in range(depth):
        lam_init = 0.8 - 0.6 * math.exp(-0.3 * layer)
        proj3 = _inproj(hn.reshape(bsz, seq, D_MODEL), w_main[layer], conv_w[layer],
                        conv_b[layer][None])
        hm = _mlstm(proj3, gates, mlstm_norm_g[layer][None])
        hd = _attn(proj3, lamv[layer], g_diff[layer], lam_init)
        hm = hm.reshape(rows, D_MLSTM)
        hd = hd.reshape(rows, D_DIFF)
        if layer + 1 < depth:
            nxt = (pre_norm_g[layer + 1][None], w_gate[layer + 1], b_gate[layer + 1])
            h, hn, gates = _outproj(hm, hd, h, w_o[layer], post_norm_g[layer][None], nxt)
        else:
            h = _outproj(hm, hd, h, w_o[layer], post_norm_g[layer][None], None)
    return h.reshape(bsz, seq, D_MODEL)[:, N_META:]
```

```python
import functools
import math

import jax
import jax.numpy as jnp
from jax import lax
from jax.experimental import pallas as pl
from jax.experimental.pallas import tpu as pltpu

D_MODEL = 1024
N_META = 16
D_MLSTM = 1024
D_DIFF = 1024
MLSTM_HEADS = 4
MLSTM_HEAD_DIM = 256
CONV_WIDTH = 4
DIFF_HEADS = 8
DIFF_HEAD_DIM = 64
DIFF_V_DIM = 128
N_GATES = 2 * MLSTM_HEADS
SECTION = 1024
D_MAIN = 9 * SECTION
EPS = 1e-6
NEG = -1e30
LOG2E = math.log2(math.e)

LANES = 128
SUBLANES = 8
BF16_ROWS = 16
GATE_PAD = LANES
MLSTM_CHUNK = 256
MLSTM_TAIL = 128
ATTN_T = 256
ATTN_RC = 64
IN_SUB = 256
CONV_ROWS = 64
VMEM_LIMIT = 48 * 1024 * 1024

F32 = jnp.float32
BF16 = jnp.bfloat16


def _row_tile(rows, cap=768, mult=BF16_ROWS):
    best = None
    for t in range(mult, cap + 1, mult):
        if rows % t == 0:
            best = t
    assert best is not None, rows
    return best


def _sigmoid(x):
    return 0.5 * jnp.tanh(0.5 * x) + 0.5


def _silu(x):
    return x * _sigmoid(x)


def _log_sigmoid(x):
    return jnp.minimum(x, 0.0) - jnp.log1p(jnp.exp(-jnp.abs(x)))


def _dot(a, b):
    return jnp.dot(a, b, preferred_element_type=F32)


def _dot_nt(a, b):
    return lax.dot_general(a, b, (((1,), (1,)), ((), ())), preferred_element_type=F32)


def _dot_tn(a, b):
    return lax.dot_general(a, b, (((0,), (0,)), ((), ())), preferred_element_type=F32)


def _prenorm_gates(h, gpre, wg, bg):
    ms = jnp.mean(h * h, axis=-1, keepdims=True)
    hn = (h * lax.rsqrt(ms + EPS) * gpre).astype(BF16)
    pre = _dot(hn, wg) + bg
    lane = lax.broadcasted_iota(jnp.int32, pre.shape, 1)
    gates = jnp.where(lane < MLSTM_HEADS, pre, _log_sigmoid(pre))
    return hn, gates


def _prenorm_kernel(h_ref, gpre_ref, wg_ref, bg_ref, hn_ref, gates_ref):
    hn, gates = _prenorm_gates(h_ref[...], gpre_ref[...], wg_ref[...], bg_ref[...])
    hn_ref[...] = hn
    gates_ref[...] = gates


def _prenorm(h, gpre, wg, bg):
    rows = h.shape[0]
    tm = _row_tile(rows)
    return pl.pallas_call(
        _prenorm_kernel,
        grid=(rows // tm,),
        in_specs=[
            pl.BlockSpec((tm, D_MODEL), lambda i: (i, 0)),
            pl.BlockSpec((1, D_MODEL), lambda i: (0, 0)),
            pl.BlockSpec((D_MODEL, GATE_PAD), lambda i: (0, 0)),
            pl.BlockSpec((1, GATE_PAD), lambda i: (0, 0)),
        ],
        out_specs=[
            pl.BlockSpec((tm, D_MODEL), lambda i: (i, 0)),
            pl.BlockSpec((tm, GATE_PAD), lambda i: (i, 0)),
        ],
        out_shape=[
            jax.ShapeDtypeStruct((rows, D_MODEL), BF16),
            jax.ShapeDtypeStruct((rows, GATE_PAD), F32),
        ],
        compiler_params=pltpu.CompilerParams(
            dimension_semantics=("arbitrary",), vmem_limit_bytes=VMEM_LIMIT),
        name="prenorm",
    )(h, gpre, wg, bg)


SEC_QM, SEC_KM, SEC_VM, SEC_OM, SEC_ZM, SEC_QD, SEC_KD, SEC_VD, SEC_ZD = range(9)
Q_D_SCALE = DIFF_HEAD_DIM ** -0.5 * LOG2E


def _inproj_kernel(x_ref, wm_ref, wd_ref, cw_ref, cb_ref, o_ref, acc_scr, *, seq):
    j = pl.program_id(0)
    x = x_ref[0]
    nsub = SECTION // IN_SUB

    def elementwise(f, w_ref):
        for c in range(nsub):
            cols = slice(c * IN_SUB, (c + 1) * IN_SUB)
            o_ref[0, :, cols] = f(_dot(x, w_ref[:, cols])).astype(BF16)

    for sec_id, f, w_ref in ((SEC_VM, lambda y: y, wm_ref), (SEC_OM, _sigmoid, wm_ref),
                             (SEC_ZM, _silu, wm_ref), (SEC_QD, lambda y: y * Q_D_SCALE, wd_ref),
                             (SEC_ZD, _silu, wd_ref)):
        pl.when(j == sec_id)(functools.partial(elementwise, f, w_ref))
    pl.when((j == SEC_KD) | (j == SEC_VD))(functools.partial(elementwise, lambda y: y, wd_ref))

    @pl.when(j <= SEC_KM)
    def _():
        qscale = jnp.where(j == SEC_QM, MLSTM_HEAD_DIM ** -0.5, 1.0).astype(F32)
        rc = _row_tile(seq, cap=CONV_ROWS)
        pad = SUBLANES

        def col_slice(c):
            return slice(c * IN_SUB, (c + 1) * IN_SUB)

        def matmul(c):
            acc_scr[c % 2, pad:pad + seq, :] = _dot(x, wm_ref[:, col_slice(c)])

        def conv(c):
            cw = 0.5 * cw_ref[:, col_slice(c)]
            cb = 0.5 * cb_ref[:, col_slice(c)]
            for r0 in range(0, seq, rc):
                hy = cb
                for d in range(CONV_WIDTH):
                    lo = pad + r0 - d
                    hy = hy + acc_scr[c % 2, lo:lo + rc, :] * cw[CONV_WIDTH - 1 - d:CONV_WIDTH - d, :]
                y = (hy * jnp.tanh(hy) + hy) * qscale
                o_ref[0, r0:r0 + rc, col_slice(c)] = y.astype(BF16)

        acc_scr[:, 0:pad, :] = jnp.zeros((2, pad, IN_SUB), F32)
        matmul(0)
        for c in range(nsub):
            if c + 1 < nsub:
                matmul(c + 1)
            conv(c)


def _inproj(hn3, w_m, w_d, conv_w, conv_b):
    bsz, seq, _ = hn3.shape
    nsec = D_MAIN // SECTION
    return pl.pallas_call(
        functools.partial(_inproj_kernel, seq=seq),
        grid=(nsec, bsz),
        in_specs=[
            pl.BlockSpec((1, seq, D_MODEL), lambda j, b: (b, 0, 0)),
            pl.BlockSpec((D_MODEL, SECTION), lambda j, b: (0, jnp.minimum(j, SEC_ZM))),
            pl.BlockSpec((D_MODEL, SECTION), lambda j, b: (0, jnp.maximum(j - SEC_QD, 0))),
            pl.BlockSpec((CONV_WIDTH, SECTION), lambda j, b: (0, jnp.minimum(j, SEC_KM))),
            pl.BlockSpec((1, SECTION), lambda j, b: (0, jnp.minimum(j, SEC_KM))),
        ],
        out_specs=pl.BlockSpec((1, seq, SECTION), lambda j, b: (b, 0, j)),
        out_shape=jax.ShapeDtypeStruct((bsz, seq, D_MAIN), BF16),
        scratch_shapes=[pltpu.VMEM((2, SUBLANES + seq, IN_SUB), F32)],
        compiler_params=pltpu.CompilerParams(
            dimension_semantics=("arbitrary", "arbitrary"), vmem_limit_bytes=VMEM_LIMIT),
        name="inproj",
    )(hn3, w_m, w_d, conv_w, conv_b)


def _mlstm_kernel(q_ref, k_ref, v_ref, o_ref, z_ref, g_ref, gr_ref, grt_ref, gn_ref, out_ref,
                  w_scr, winter_scr, floor_scr, wk_scr, dec_scr, vx_scr, ctx_scr, *, seq):
    hd = MLSTM_HEAD_DIM
    nfull = seq // MLSTM_CHUNK
    tail = seq - nfull * MLSTM_CHUNK
    chunks = [(c * MLSTM_CHUNK, MLSTM_CHUNK, 0, gr_ref[0, 0, :, c * MLSTM_CHUNK:(c + 1) * MLSTM_CHUNK])
              for c in range(nfull)]
    if tail:
        chunks.append((seq - MLSTM_TAIL, MLSTM_TAIL, MLSTM_TAIL - tail, grt_ref[0, 0]))

    vx_scr[:, 0:hd] = v_ref[0]
    vx_scr[:, hd:] = jnp.ones((seq, LANES), BF16)
    ctx_scr[...] = jnp.zeros_like(ctx_scr)

    m_prev = jnp.zeros((1, 1), F32)
    for ci, (start, lc, row_lo, gr) in enumerate(chunks):
        g = g_ref[0, 0, start:start + lc, :]
        li_col, lf_col = g[:, 0:1], g[:, 1:2]
        li_row, lf_row = gr[0:1, :], gr[1:2, :]
        if row_lo > 0:
            live_c = lax.broadcasted_iota(jnp.int32, (lc, 1), 0) >= row_lo
            live_r = lax.broadcasted_iota(jnp.int32, (1, lc), 1) >= row_lo
            li_col = jnp.where(live_c, li_col, NEG)
            lf_col = jnp.where(live_c, lf_col, 0.0)
            li_row = jnp.where(live_r, li_row, NEG)
            lf_row = jnp.where(live_r, lf_row, 0.0)
        ir = lax.broadcasted_iota(jnp.int32, (lc, lc), 0)
        ic = lax.broadcasted_iota(jnp.int32, (lc, lc), 1)
        causal = ic <= ir
        b_col = jnp.sum(jnp.where(causal, lf_row, 0.0), axis=1, keepdims=True)
        b_row = jnp.sum(jnp.where(ir <= ic, lf_col, 0.0), axis=0, keepdims=True)
        c_row = li_row - b_row
        c_col = li_col - b_col
        cmax = jnp.max(jnp.where(causal, c_row, NEG), axis=1, keepdims=True)
        m_col = jnp.maximum(m_prev, cmax)
        w_scr[ci, 0:lc, 0:lc] = jnp.exp(jnp.where(causal, c_row - m_col, NEG))
        m_rep = jnp.broadcast_to(m_col, (lc, LANES))
        winter_scr[ci, 0:lc, :] = jnp.exp(m_prev - m_rep)
        floor_scr[ci, 0:lc, :] = jnp.exp(-jnp.broadcast_to(b_col, (lc, LANES)) - m_rep)
        m_last = m_col[lc - 1:lc, :]
        wk_scr[ci, 0:lc, :] = jnp.broadcast_to(jnp.exp(c_col - m_last), (lc, LANES))
        dec_scr[ci] = jnp.broadcast_to(jnp.exp(m_prev - m_last), (1, LANES))
        m_prev = b_row[:, lc - 1:lc] + m_last

    for ci, (start, lc, row_lo, _) in enumerate(chunks):
        rows = slice(start, start + lc)
        qb = q_ref[0, rows, :]
        kb = k_ref[0, rows, :]
        vxb = vx_scr[rows, :]
        s = _dot_nt(qb, kb) * w_scr[ci, 0:lc, 0:lc]
        ctx = ctx_scr[...]
        w_inter = winter_scr[ci, 0:lc, :]
        ext = (_dot(s.astype(BF16), vxb)
               + jnp.concatenate([w_inter] * 3, axis=1) * _dot(qb, ctx.astype(BF16)))
        den = ext[:, hd:]
        rinv = 1.0 / jnp.maximum(jnp.abs(den), floor_scr[ci, 0:lc, :])
        h = ext[:, 0:hd] * jnp.concatenate([rinv] * 2, axis=1)

        wk = wk_scr[ci, 0:lc, :]
        kw = (kb.astype(F32) * jnp.concatenate([wk] * 2, axis=1)).astype(BF16)
        ctx_scr[...] = jnp.concatenate([dec_scr[ci]] * 3, axis=1) * ctx + _dot_tn(kw, vxb)

        hm = h * o_ref[0, rows, :].astype(F32)
        ms = jnp.mean(hm * hm, axis=1, keepdims=True)
        y = (hm * lax.rsqrt(ms + EPS) * gn_ref[...] * z_ref[0, rows, :].astype(F32)).astype(BF16)
        out_ref[0, start + row_lo:start + lc, :] = y[row_lo:, :]


def _mlstm(proj3, gates, g_mlstm):
    bsz, seq, _ = proj3.shape
    nfull = seq // MLSTM_CHUNK
    tail = seq % MLSTM_CHUNK
    assert nfull >= 1 and tail % BF16_ROWS == 0 and tail <= MLSTM_TAIL
    hd = MLSTM_HEAD_DIM
    nchunk = nfull + (1 if tail else 0)

    g4 = gates[:, :N_GATES].reshape(bsz, seq, 2, MLSTM_HEADS)
    g_col = g4.transpose(0, 3, 1, 2)
    g_row = g4.transpose(0, 3, 2, 1)
    g_row_tail = g_row[..., seq - MLSTM_TAIL:]

    def sec(s):
        return pl.BlockSpec((1, seq, hd), lambda b, h, s=s: (b, 0, s * MLSTM_HEADS + h))

    return pl.pallas_call(
        functools.partial(_mlstm_kernel, seq=seq),
        grid=(bsz, MLSTM_HEADS),
        in_specs=[
            sec(SEC_QM), sec(SEC_KM), sec(SEC_VM), sec(SEC_OM), sec(SEC_ZM),
            pl.BlockSpec((1, 1, seq, 2), lambda b, h: (b, h, 0, 0)),
            pl.BlockSpec((1, 1, 2, seq), lambda b, h: (b, h, 0, 0)),
            pl.BlockSpec((1, 1, 2, MLSTM_TAIL), lambda b, h: (b, h, 0, 0)),
            pl.BlockSpec((1, hd), lambda b, h: (0, h)),
        ],
        out_specs=pl.BlockSpec((1, seq, hd), lambda b, h: (b, 0, h)),
        out_shape=jax.ShapeDtypeStruct((bsz, seq, D_MLSTM), BF16),
        scratch_shapes=[
            pltpu.VMEM((nchunk, MLSTM_CHUNK, MLSTM_CHUNK), F32),
            pltpu.VMEM((nchunk, MLSTM_CHUNK, LANES), F32),
            pltpu.VMEM((nchunk, MLSTM_CHUNK, LANES), F32),
            pltpu.VMEM((nchunk, MLSTM_CHUNK, LANES), F32),
            pltpu.VMEM((nchunk, 1, LANES), F32),
            pltpu.VMEM((seq, hd + LANES), BF16),
            pltpu.VMEM((hd, hd + LANES), F32),
        ],
        compiler_params=pltpu.CompilerParams(
            dimension_semantics=("arbitrary", "arbitrary"), vmem_limit_bytes=VMEM_LIMIT),
        name="mlstm",
    )(proj3, proj3, proj3, proj3, proj3, g_col, g_row, g_row_tail, g_mlstm)


def _attn_kernel(q_ref, k_ref, v_ref, z_ref, lamv_ref, gd_ref, out_ref,
                 s_scr, p_scr, bias_scr, vx_scr, *, seq, lam_init):
    t = ATTN_T
    nq = seq // t
    tail = seq - nq * t

    lamv = lamv_ref[...]
    lam = (jnp.exp(jnp.sum(lamv[0:1] * lamv[1:2], axis=1, keepdims=True))
           - jnp.exp(jnp.sum(lamv[2:3] * lamv[3:4], axis=1, keepdims=True)) + lam_init)
    out_gain = gd_ref[0] * (1.0 - lam_init)

    r = lax.broadcasted_iota(jnp.int32, (2 * t, t), 0)
    c = lax.broadcasted_iota(jnp.int32, (2 * t, t), 1)
    bias_scr[...] = jnp.where(c <= jnp.where(r >= t, r - t, r), 0.0, NEG)
    vx_scr[:, 0:DIFF_V_DIM] = v_ref[0]
    vx_scr[:, DIFF_V_DIM:] = jnp.ones((seq, DIFF_V_DIM), BF16)

    blocks = [(i * t, t, [(j * t, None) for j in range(i)] + [(i * t, "diag")]) for i in range(nq)]
    if tail:
        blocks.append((nq * t, tail, [(j * t, None) for j in range(nq)] + [(seq - t, "tail")]))

    def scores(bi):
        qstart, n, tiles = blocks[bi]
        qf = q_ref[0, qstart:qstart + n, :].astype(F32)
        first = lax.broadcasted_iota(jnp.int32, (n, 2 * DIFF_HEAD_DIM), 1) < DIFF_HEAD_DIM
        qs = jnp.concatenate([jnp.where(first, qf, 0.0), jnp.where(first, 0.0, qf)],
                             axis=0).astype(BF16)
        for ti, (kstart, kind) in enumerate(tiles):
            s = _dot_nt(qs, k_ref[0, kstart:kstart + t, :])
            if kind == "diag":
                s = s + bias_scr[...]
            elif kind == "tail":
                rr = lax.broadcasted_iota(jnp.int32, (2 * n, t), 0)
                qpos = qstart + jnp.where(rr >= n, rr - n, rr)
                kpos = kstart + lax.broadcasted_iota(jnp.int32, (2 * n, t), 1)
                s = jnp.where((kpos <= qpos) & (kpos >= qstart), s, NEG)
            s_scr[bi % 2, 0:2 * n, ti * t:(ti + 1) * t] = s

    def softmax(bi):
        _, n, tiles = blocks[bi]
        buf = bi % 2
        rc = min(ATTN_RC, 2 * n)
        ncol = len(tiles) * t // LANES
        for r0 in range(0, 2 * n, rc):
            rows = slice(r0, r0 + rc)
            mx = s_scr[buf, rows, 0:LANES]
            for ci in range(1, ncol):
                mx = jnp.maximum(mx, s_scr[buf, rows, ci * LANES:(ci + 1) * LANES])
            m = jnp.broadcast_to(jnp.max(mx, axis=1, keepdims=True), (rc, LANES))
            for ci in range(ncol):
                cols = slice(ci * LANES, (ci + 1) * LANES)
                p_scr[buf, rows, cols] = jnp.exp2(s_scr[buf, rows, cols] - m).astype(BF16)

    def finish(bi):
        qstart, n, tiles = blocks[bi]
        buf = bi % 2
        nfull = len(tiles) - 1
        last_k = tiles[-1][0]
        o = []
        for half in range(2):
            rows = slice(half * n, (half + 1) * n)
            acc = _dot(p_scr[buf, rows, nfull * t:(nfull + 1) * t], vx_scr[last_k:last_k + t, :])
            if nfull:
                acc = acc + _dot(p_scr[buf, rows, 0:nfull * t], vx_scr[0:nfull * t, :])
            o.append(acc[:, 0:DIFF_V_DIM] / acc[:, DIFF_V_DIM:])
        a = o[0] - lam * o[1]
        ms = jnp.mean(a * a, axis=1, keepdims=True)
        y = a * lax.rsqrt(ms + EPS) * out_gain * z_ref[0, qstart:qstart + n, :].astype(F32)
        out_ref[0, qstart:qstart + n, :] = y.astype(BF16)

    scores(0)
    for bi in range(len(blocks)):
        if bi + 1 < len(blocks):
            scores(bi + 1)
        softmax(bi)
        finish(bi)


def _attn(proj3, lamv, g_diff3, lam_init):
    bsz, seq, _ = proj3.shape
    assert seq >= ATTN_T and (seq % ATTN_T) % BF16_ROWS == 0
    w = DIFF_V_DIM
    width = -(-seq // ATTN_T) * ATTN_T
    per_sec = SECTION // w

    def sec(s):
        return pl.BlockSpec((1, seq, w), lambda b, h, s=s: (b, 0, s * per_sec + h))

    return pl.pallas_call(
        functools.partial(_attn_kernel, seq=seq, lam_init=lam_init),
        grid=(bsz, DIFF_HEADS),
        in_specs=[
            sec(SEC_QD), sec(SEC_KD), sec(SEC_VD), sec(SEC_ZD),
            pl.BlockSpec((4, DIFF_HEAD_DIM), lambda b, h: (0, 0)),
            pl.BlockSpec((1, 1, w), lambda b, h: (h, 0, 0)),
        ],
        out_specs=pl.BlockSpec((1, seq, w), lambda b, h: (b, 0, h)),
        out_shape=jax.ShapeDtypeStruct((bsz, seq, D_DIFF), BF16),
        scratch_shapes=[
            pltpu.VMEM((2, 2 * ATTN_T, width), F32),
            pltpu.VMEM((2, 2 * ATTN_T, width), BF16),
            pltpu.VMEM((2 * ATTN_T, ATTN_T), F32),
            pltpu.VMEM((seq, 2 * w), BF16),
        ],
        compiler_params=pltpu.CompilerParams(
            dimension_semantics=("arbitrary", "arbitrary"), vmem_limit_bytes=VMEM_LIMIT),
        name="diffattn",
    )(proj3, proj3, proj3, proj3, lamv, g_diff3)


def _outproj_body(hm_ref, hd_ref, h_ref, wo_ref, gpost_ref):
    y = _dot(hm_ref[...], wo_ref[:D_MLSTM, :]) + _dot(hd_ref[...], wo_ref[D_MLSTM:, :])
    ms = jnp.mean(y * y, axis=-1, keepdims=True)
    return h_ref[...] + y * lax.rsqrt(ms + EPS) * gpost_ref[...]


def _outproj_next_kernel(hm_ref, hd_ref, h_ref, wo_ref, gpost_ref, gpre_ref, wg_ref, bg_ref,
                         hnew_ref, hn_ref, gates_ref):
    hnew = _outproj_body(hm_ref, hd_ref, h_ref, wo_ref, gpost_ref)
    hnew_ref[...] = hnew
    hn, gates = _prenorm_gates(hnew, gpre_ref[...], wg_ref[...], bg_ref[...])
    hn_ref[...] = hn
    gates_ref[...] = gates


def _outproj_last_kernel(hm_ref, hd_ref, h_ref, wo_ref, gpost_ref, hnew_ref):
    y = _dot(hm_ref[0], wo_ref[:D_MLSTM, :]) + _dot(hd_ref[0], wo_ref[D_MLSTM:, :])
    ms = jnp.mean(y * y, axis=-1, keepdims=True)
    hnew_ref[0] = h_ref[0] + y * lax.rsqrt(ms + EPS) * gpost_ref[...]


def _outproj(hm, hd, h, wo, gpost, nxt, nxt_shape=None):
    rows = h.shape[0]
    tm = _row_tile(rows)
    row_spec = lambda width: pl.BlockSpec((tm, width), lambda i: (i, 0))
    const = lambda shape: pl.BlockSpec(shape, lambda i: (0, 0))
    in_specs = [row_spec(D_MLSTM), row_spec(D_DIFF), row_spec(D_MODEL),
                const((D_MLSTM + D_DIFF, D_MODEL)), const((1, D_MODEL))]
    params = pltpu.CompilerParams(dimension_semantics=("arbitrary",), vmem_limit_bytes=VMEM_LIMIT)
    if nxt is None:
        bsz, s_real = nxt_shape
        seq = rows // bsz
        tr = _row_tile(s_real, cap=512)

        def real_rows(width):
            return pl.BlockSpec((pl.Element(1), pl.Element(tr), pl.Element(width)),
                                lambda b, i: (b, pl.multiple_of(N_META + i * tr, BF16_ROWS), 0))

        const2 = lambda shape: pl.BlockSpec(shape, lambda b, i: (0, 0))
        return pl.pallas_call(
            _outproj_last_kernel,
            grid=(bsz, s_real // tr),
            in_specs=[real_rows(D_MLSTM), real_rows(D_DIFF), real_rows(D_MODEL),
                      const2((D_MLSTM + D_DIFF, D_MODEL)), const2((1, D_MODEL))],
            out_specs=pl.BlockSpec((1, tr, D_MODEL), lambda b, i: (b, i, 0)),
            out_shape=jax.ShapeDtypeStruct((bsz, s_real, D_MODEL), F32),
            compiler_params=pltpu.CompilerParams(
                dimension_semantics=("arbitrary", "arbitrary"), vmem_limit_bytes=VMEM_LIMIT),
            name="outproj_last",
        )(hm.reshape(bsz, seq, D_MLSTM), hd.reshape(bsz, seq, D_DIFF), h.reshape(bsz, seq, D_MODEL),
          wo, gpost)
    gpre, wg, bg = nxt
    return pl.pallas_call(
        _outproj_next_kernel,
        grid=(rows // tm,),
        in_specs=in_specs + [const((1, D_MODEL)), const((D_MODEL, GATE_PAD)), const((1, GATE_PAD))],
        out_specs=[row_spec(D_MODEL), row_spec(D_MODEL), row_spec(GATE_PAD)],
        out_shape=[
            jax.ShapeDtypeStruct((rows, D_MODEL), F32),
            jax.ShapeDtypeStruct((rows, D_MODEL), BF16),
            jax.ShapeDtypeStruct((rows, GATE_PAD), F32),
        ],
        compiler_params=params,
        name="outproj",
    )(hm, hd, h, wo, gpost, gpre, wg, bg)


def kernel(x, meta_tokens, pre_norm_g, post_norm_g, w_in, b_gates, conv_w, conv_b, mlstm_norm_g,
           lambda_q1, lambda_k1, lambda_q2, lambda_k2, diff_norm_g, w_out):
    bsz, s_real, _ = x.shape
    depth = w_in.shape[0]
    seq = s_real + N_META
    rows = bsz * seq

    g0 = 5 * D_MLSTM
    w_m = w_in[:, :, :g0].astype(BF16)
    w_d = w_in[:, :, g0 + N_GATES:].astype(BF16)
    w_gate = jnp.pad(w_in[:, :, g0:g0 + N_GATES],
                     ((0, 0), (0, 0), (0, GATE_PAD - N_GATES))).astype(BF16)
    b_gate = jnp.pad(b_gates, ((0, 0), (0, GATE_PAD - N_GATES)))[:, None, :]
    w_o = w_out.astype(BF16)
    lamv = jnp.stack([lambda_q1, lambda_k1, lambda_q2, lambda_k2], axis=1)
    g_diff = diff_norm_g.reshape(depth, DIFF_HEADS, 1, DIFF_V_DIM)

    meta = jnp.broadcast_to(meta_tokens[None].astype(x.dtype), (bsz, N_META, D_MODEL))
    h = jnp.concatenate([meta, x], axis=1).reshape(rows, D_MODEL)

    hn, gates = _prenorm(h, pre_norm_g[0][None], w_gate[0], b_gate[0])
    for layer in range(depth):
        lam_init = 0.8 - 0.6 * math.exp(-0.3 * layer)
        proj3 = _inproj(hn.reshape(bsz, seq, D_MODEL), w_m[layer], w_d[layer], conv_w[layer],
                        conv_b[layer][None])
        hm = _mlstm(proj3, gates, mlstm_norm_g[layer][None])
        hd = _attn(proj3, lamv[layer], g_diff[layer], lam_init)
        hm = hm.reshape(rows, D_MLSTM)
        hd = hd.reshape(rows, D_DIFF)
        if layer + 1 < depth:
            nxt = (pre_norm_g[layer + 1][None], w_gate[layer + 1], b_gate[layer + 1])
            h, hn, gates = _outproj(hm, hd, h, w_o[layer], post_norm_g[layer][None], nxt)
        else:
            out = _outproj(hm, hd, h, w_o[layer], post_norm_g[layer][None], None, (bsz, s_real))
    return out
```

```python
import functools
import math

import jax
import jax.numpy as jnp
from jax import lax
from jax.experimental import pallas as pl
from jax.experimental.pallas import tpu as pltpu

D_MODEL = 1024
N_META = 16
D_MLSTM = 1024
D_DIFF = 1024
MLSTM_HEADS = 4
MLSTM_HEAD_DIM = 256
CONV_WIDTH = 4
DIFF_HEADS = 8
DIFF_HEAD_DIM = 64
DIFF_V_DIM = 128
N_GATES = 2 * MLSTM_HEADS
SECTION = 1024
D_MAIN = 9 * SECTION
EPS = 1e-6
NEG = -1e30
LOG2E = math.log2(math.e)

LANES = 128
SUBLANES = 8
BF16_ROWS = 16
GATE_PAD = LANES
MLSTM_CHUNK = 256
MLSTM_TAIL = 128
ATTN_T = 256
ATTN_RC = 64
IN_SUB = 256
CONV_ROWS = 768
VMEM_LIMIT = 48 * 1024 * 1024

F32 = jnp.float32
BF16 = jnp.bfloat16


def _row_tile(rows, cap=768, mult=BF16_ROWS):
    best = None
    for t in range(mult, cap + 1, mult):
        if rows % t == 0:
            best = t
    assert best is not None, rows
    return best


def _sigmoid(x):
    return 0.5 * jnp.tanh(0.5 * x) + 0.5


def _silu(x):
    return x * _sigmoid(x)


def _log_sigmoid(x):
    return jnp.minimum(x, 0.0) - jnp.log1p(jnp.exp(-jnp.abs(x)))


def _dot(a, b):
    return jnp.dot(a, b, preferred_element_type=F32)


def _dot_nt(a, b):
    return lax.dot_general(a, b, (((1,), (1,)), ((), ())), preferred_element_type=F32)


def _dot_tn(a, b):
    return lax.dot_general(a, b, (((0,), (0,)), ((), ())), preferred_element_type=F32)


def _prenorm_gates(h, gpre, wg, bg):
    ms = jnp.mean(h * h, axis=-1, keepdims=True)
    hn = (h * lax.rsqrt(ms + EPS) * gpre).astype(BF16)
    pre = _dot(hn, wg) + bg
    lane = lax.broadcasted_iota(jnp.int32, pre.shape, 1)
    gates = jnp.where(lane < MLSTM_HEADS, pre, _log_sigmoid(pre))
    return hn, gates


def _prenorm_kernel(h_ref, gpre_ref, wg_ref, bg_ref, hn_ref, gates_ref):
    hn, gates = _prenorm_gates(h_ref[...], gpre_ref[...], wg_ref[...], bg_ref[...])
    hn_ref[...] = hn
    gates_ref[...] = gates


def _prenorm(h, gpre, wg, bg):
    rows = h.shape[0]
    tm = _row_tile(rows)
    return pl.pallas_call(
        _prenorm_kernel,
        grid=(rows // tm,),
        in_specs=[
            pl.BlockSpec((tm, D_MODEL), lambda i: (i, 0)),
            pl.BlockSpec((1, D_MODEL), lambda i: (0, 0)),
            pl.BlockSpec((D_MODEL, GATE_PAD), lambda i: (0, 0)),
            pl.BlockSpec((1, GATE_PAD), lambda i: (0, 0)),
        ],
        out_specs=[
            pl.BlockSpec((tm, D_MODEL), lambda i: (i, 0)),
            pl.BlockSpec((tm, GATE_PAD), lambda i: (i, 0)),
        ],
        out_shape=[
            jax.ShapeDtypeStruct((rows, D_MODEL), BF16),
            jax.ShapeDtypeStruct((rows, GATE_PAD), F32),
        ],
        compiler_params=pltpu.CompilerParams(
            dimension_semantics=("arbitrary",), vmem_limit_bytes=VMEM_LIMIT),
        name="prenorm",
    )(h, gpre, wg, bg)


SEC_QM, SEC_KM, SEC_VM, SEC_OM, SEC_ZM, SEC_QD, SEC_KD, SEC_VD, SEC_ZD = range(9)
Q_D_SCALE = DIFF_HEAD_DIM ** -0.5 * LOG2E


def _inproj_kernel(x_ref, wm_ref, wd_ref, cw_ref, cb_ref, o_ref, acc_scr, *, seq):
    j = pl.program_id(0)
    x = x_ref[0]
    nsub = SECTION // IN_SUB

    def elementwise(f, w_ref):
        for c in range(nsub):
            cols = slice(c * IN_SUB, (c + 1) * IN_SUB)
            o_ref[0, :, cols] = f(_dot(x, w_ref[:, cols])).astype(BF16)

    for sec_id, f, w_ref in ((SEC_VM, lambda y: y, wm_ref), (SEC_OM, _sigmoid, wm_ref),
                             (SEC_ZM, _silu, wm_ref), (SEC_QD, lambda y: y * Q_D_SCALE, wd_ref),
                             (SEC_ZD, _silu, wd_ref)):
        pl.when(j == sec_id)(functools.partial(elementwise, f, w_ref))
    pl.when((j == SEC_KD) | (j == SEC_VD))(functools.partial(elementwise, lambda y: y, wd_ref))

    @pl.when(j <= SEC_KM)
    def _():
        qscale = jnp.where(j == SEC_QM, MLSTM_HEAD_DIM ** -0.5, 1.0).astype(F32)
        rc = _row_tile(seq, cap=CONV_ROWS)
        pad = SUBLANES

        def col_slice(c):
            return slice(c * IN_SUB, (c + 1) * IN_SUB)

        def matmul(c):
            acc_scr[c % 2, pad:pad + seq, :] = _dot(x, wm_ref[:, col_slice(c)])

        def conv(c):
            cw = 0.5 * cw_ref[:, col_slice(c)]
            cb = 0.5 * cb_ref[:, col_slice(c)]
            for r0 in range(0, seq, rc):
                hy = cb
                for d in range(CONV_WIDTH):
                    lo = pad + r0 - d
                    hy = hy + acc_scr[c % 2, lo:lo + rc, :] * cw[CONV_WIDTH - 1 - d:CONV_WIDTH - d, :]
                y = (hy * jnp.tanh(hy) + hy) * qscale
                o_ref[0, r0:r0 + rc, col_slice(c)] = y.astype(BF16)

        acc_scr[:, 0:pad, :] = jnp.zeros((2, pad, IN_SUB), F32)
        matmul(0)
        for c in range(nsub):
            if c + 1 < nsub:
                matmul(c + 1)
            conv(c)


def _inproj(hn3, w_m, w_d, conv_w, conv_b):
    bsz, seq, _ = hn3.shape
    nsec = D_MAIN // SECTION
    return pl.pallas_call(
        functools.partial(_inproj_kernel, seq=seq),
        grid=(nsec, bsz),
        in_specs=[
            pl.BlockSpec((1, seq, D_MODEL), lambda j, b: (b, 0, 0)),
            pl.BlockSpec((D_MODEL, SECTION), lambda j, b: (0, jnp.minimum(j, SEC_ZM))),
            pl.BlockSpec((D_MODEL, SECTION), lambda j, b: (0, jnp.maximum(j - SEC_QD, 0))),
            pl.BlockSpec((CONV_WIDTH, SECTION), lambda j, b: (0, jnp.minimum(j, SEC_KM))),
            pl.BlockSpec((1, SECTION), lambda j, b: (0, jnp.minimum(j, SEC_KM))),
        ],
        out_specs=pl.BlockSpec((1, seq, SECTION), lambda j, b: (b, 0, j)),
        out_shape=jax.ShapeDtypeStruct((bsz, seq, D_MAIN), BF16),
        scratch_shapes=[pltpu.VMEM((2, SUBLANES + seq, IN_SUB), F32)],
        compiler_params=pltpu.CompilerParams(
            dimension_semantics=("arbitrary", "arbitrary"), vmem_limit_bytes=VMEM_LIMIT),
        name="inproj",
    )(hn3, w_m, w_d, conv_w, conv_b)


def _mlstm_kernel(q_ref, k_ref, v_ref, o_ref, z_ref, g_ref, gr_ref, grt_ref, gn_ref, out_ref,
                  w_scr, winter_scr, floor_scr, wk_scr, dec_scr, vx_scr, ctx_scr, *, seq):
    hd = MLSTM_HEAD_DIM
    nfull = seq // MLSTM_CHUNK
    tail = seq - nfull * MLSTM_CHUNK
    chunks = [(c * MLSTM_CHUNK, MLSTM_CHUNK, 0, gr_ref[0, 0, :, c * MLSTM_CHUNK:(c + 1) * MLSTM_CHUNK])
              for c in range(nfull)]
    if tail:
        chunks.append((seq - MLSTM_TAIL, MLSTM_TAIL, MLSTM_TAIL - tail, grt_ref[0, 0]))

    vx_scr[:, 0:hd] = v_ref[0]
    vx_scr[:, hd:] = jnp.ones((seq, LANES), BF16)
    ctx_scr[...] = jnp.zeros_like(ctx_scr)

    m_prev = jnp.zeros((1, 1), F32)
    for ci, (start, lc, row_lo, gr) in enumerate(chunks):
        g = g_ref[0, 0, start:start + lc, :]
        li_col, lf_col = g[:, 0:1], g[:, 1:2]
        li_row, lf_row = gr[0:1, :], gr[1:2, :]
        if row_lo > 0:
            live_c = lax.broadcasted_iota(jnp.int32, (lc, 1), 0) >= row_lo
            live_r = lax.broadcasted_iota(jnp.int32, (1, lc), 1) >= row_lo
            li_col = jnp.where(live_c, li_col, NEG)
            lf_col = jnp.where(live_c, lf_col, 0.0)
            li_row = jnp.where(live_r, li_row, NEG)
            lf_row = jnp.where(live_r, lf_row, 0.0)
        ir = lax.broadcasted_iota(jnp.int32, (lc, lc), 0)
        ic = lax.broadcasted_iota(jnp.int32, (lc, lc), 1)
        causal = ic <= ir
        b_col = jnp.sum(jnp.where(causal, lf_row, 0.0), axis=1, keepdims=True)
        b_row = jnp.sum(jnp.where(ir <= ic, lf_col, 0.0), axis=0, keepdims=True)
        c_row = li_row - b_row
        c_col = li_col - b_col
        cmax = jnp.max(jnp.where(causal, c_row, NEG), axis=1, keepdims=True)
        m_col = jnp.maximum(m_prev, cmax)
        w_scr[ci, 0:lc, 0:lc] = jnp.exp(jnp.where(causal, c_row - m_col, NEG))
        m_rep = jnp.broadcast_to(m_col, (lc, LANES))
        winter_scr[ci, 0:lc, :] = jnp.exp(m_prev - m_rep)
        floor_scr[ci, 0:lc, :] = jnp.exp(-jnp.broadcast_to(b_col, (lc, LANES)) - m_rep)
        m_last = m_col[lc - 1:lc, :]
        wk_scr[ci, 0:lc, :] = jnp.broadcast_to(jnp.exp(c_col - m_last), (lc, LANES))
        dec_scr[ci] = jnp.broadcast_to(jnp.exp(m_prev - m_last), (1, LANES))
        m_prev = b_row[:, lc - 1:lc] + m_last

    for ci, (start, lc, row_lo, _) in enumerate(chunks):
        rows = slice(start, start + lc)
        qb = q_ref[0, rows, :]
        kb = k_ref[0, rows, :]
        vxb = vx_scr[rows, :]
        s = _dot_nt(qb, kb) * w_scr[ci, 0:lc, 0:lc]
        ctx = ctx_scr[...]
        w_inter = winter_scr[ci, 0:lc, :]
        ext = (_dot(s.astype(BF16), vxb)
               + jnp.concatenate([w_inter] * 3, axis=1) * _dot(qb, ctx.astype(BF16)))
        den = ext[:, hd:]
        rinv = 1.0 / jnp.maximum(jnp.abs(den), floor_scr[ci, 0:lc, :])
        h = ext[:, 0:hd] * jnp.concatenate([rinv] * 2, axis=1)

        wk = wk_scr[ci, 0:lc, :]
        kw = (kb.astype(F32) * jnp.concatenate([wk] * 2, axis=1)).astype(BF16)
        ctx_scr[...] = jnp.concatenate([dec_scr[ci]] * 3, axis=1) * ctx + _dot_tn(kw, vxb)

        hm = h * o_ref[0, rows, :].astype(F32)
        ms = jnp.mean(hm * hm, axis=1, keepdims=True)
        y = (hm * lax.rsqrt(ms + EPS) * gn_ref[...] * z_ref[0, rows, :].astype(F32)).astype(BF16)
        out_ref[0, start + row_lo:start + lc, :] = y[row_lo:, :]


def _mlstm(proj3, gates, g_mlstm):
    bsz, seq, _ = proj3.shape
    nfull = seq // MLSTM_CHUNK
    tail = seq % MLSTM_CHUNK
    assert nfull >= 1 and tail % BF16_ROWS == 0 and tail <= MLSTM_TAIL
    hd = MLSTM_HEAD_DIM
    nchunk = nfull + (1 if tail else 0)

    g4 = gates[:, :N_GATES].reshape(bsz, seq, 2, MLSTM_HEADS)
    g_col = g4.transpose(0, 3, 1, 2)
    g_row = g4.transpose(0, 3, 2, 1)
    g_row_tail = g_row[..., seq - MLSTM_TAIL:]

    def sec(s):
        return pl.BlockSpec((1, seq, hd), lambda b, h, s=s: (b, 0, s * MLSTM_HEADS + h))

    return pl.pallas_call(
        functools.partial(_mlstm_kernel, seq=seq),
        grid=(bsz, MLSTM_HEADS),
        in_specs=[
            sec(SEC_QM), sec(SEC_KM), sec(SEC_VM), sec(SEC_OM), sec(SEC_ZM),
            pl.BlockSpec((1, 1, seq, 2), lambda b, h: (b, h, 0, 0)),
            pl.BlockSpec((1, 1, 2, seq), lambda b, h: (b, h, 0, 0)),
            pl.BlockSpec((1, 1, 2, MLSTM_TAIL), lambda b, h: (b, h, 0, 0)),
            pl.BlockSpec((1, hd), lambda b, h: (0, h)),
        ],
        out_specs=pl.BlockSpec((1, seq, hd), lambda b, h: (b, 0, h)),
        out_shape=jax.ShapeDtypeStruct((bsz, seq, D_MLSTM), BF16),
        scratch_shapes=[
            pltpu.VMEM((nchunk, MLSTM_CHUNK, MLSTM_CHUNK), F32),
            pltpu.VMEM((nchunk, MLSTM_CHUNK, LANES), F32),
            pltpu.VMEM((nchunk, MLSTM_CHUNK, LANES), F32),
            pltpu.VMEM((nchunk, MLSTM_CHUNK, LANES), F32),
            pltpu.VMEM((nchunk, 1, LANES), F32),
            pltpu.VMEM((seq, hd + LANES), BF16),
            pltpu.VMEM((hd, hd + LANES), F32),
        ],
        compiler_params=pltpu.CompilerParams(
            dimension_semantics=("arbitrary", "arbitrary"), vmem_limit_bytes=VMEM_LIMIT),
        name="mlstm",
    )(proj3, proj3, proj3, proj3, proj3, g_col, g_row, g_row_tail, g_mlstm)


def _attn_kernel(q_ref, k_ref, v_ref, z_ref, lamv_ref, gd_ref, out_ref,
                 s_scr, p_scr, bias_scr, vx_scr, *, seq, lam_init):
    t = ATTN_T
    nq = seq // t
    tail = seq - nq * t

    lamv = lamv_ref[...]
    lam = (jnp.exp(jnp.sum(lamv[0:1] * lamv[1:2], axis=1, keepdims=True))
           - jnp.exp(jnp.sum(lamv[2:3] * lamv[3:4], axis=1, keepdims=True)) + lam_init)
    out_gain = gd_ref[0] * (1.0 - lam_init)

    r = lax.broadcasted_iota(jnp.int32, (2 * t, t), 0)
    c = lax.broadcasted_iota(jnp.int32, (2 * t, t), 1)
    bias_scr[...] = jnp.where(c <= jnp.where(r >= t, r - t, r), 0.0, NEG)
    vx_scr[:, 0:DIFF_V_DIM] = v_ref[0]
    vx_scr[:, DIFF_V_DIM:] = jnp.ones((seq, DIFF_V_DIM), BF16)

    blocks = [(i * t, t, [(j * t, None) for j in range(i)] + [(i * t, "diag")]) for i in range(nq)]
    if tail:
        last = (nq - 1) * t
        blocks[-1] = (last, t + tail,
                      [(j * t, None) for j in range(nq - 1)] + [(last, "mask"), (seq - t, "mask")])

    def scores(bi):
        qstart, n, tiles = blocks[bi]
        qf = q_ref[0, qstart:qstart + n, :].astype(F32)
        first = lax.broadcasted_iota(jnp.int32, (n, 2 * DIFF_HEAD_DIM), 1) < DIFF_HEAD_DIM
        qs = jnp.concatenate([jnp.where(first, qf, 0.0), jnp.where(first, 0.0, qf)],
                             axis=0).astype(BF16)
        for ti, (kstart, kind) in enumerate(tiles):
            s = _dot_nt(qs, k_ref[0, kstart:kstart + t, :])
            if kind == "diag":
                s = s + bias_scr[...]
            elif kind == "mask":
                rr = lax.broadcasted_iota(jnp.int32, (2 * n, t), 0)
                qpos = qstart + jnp.where(rr >= n, rr - n, rr)
                kpos = kstart + lax.broadcasted_iota(jnp.int32, (2 * n, t), 1)
                lo = kstart if ti + 1 < len(tiles) else nq * t
                s = jnp.where((kpos <= qpos) & (kpos >= lo), s, NEG)
            s_scr[bi % 2, 0:2 * n, ti * t:(ti + 1) * t] = s

    def softmax(bi):
        _, n, tiles = blocks[bi]
        buf = bi % 2
        rc = ATTN_RC if (2 * n) % ATTN_RC == 0 else BF16_ROWS * 2
        ncol = len(tiles) * t // LANES
        for r0 in range(0, 2 * n, rc):
            rows = slice(r0, r0 + rc)
            mx = s_scr[buf, rows, 0:LANES]
            for ci in range(1, ncol):
                mx = jnp.maximum(mx, s_scr[buf, rows, ci * LANES:(ci + 1) * LANES])
            m = jnp.broadcast_to(jnp.max(mx, axis=1, keepdims=True), (rc, LANES))
            for ci in range(ncol):
                cols = slice(ci * LANES, (ci + 1) * LANES)
                p_scr[buf, rows, cols] = jnp.exp2(s_scr[buf, rows, cols] - m).astype(BF16)

    def finish(bi):
        qstart, n, tiles = blocks[bi]
        buf = bi % 2
        nfull = len(tiles) - 1
        last_k = tiles[-1][0]
        o = []
        for half in range(2):
            rows = slice(half * n, (half + 1) * n)
            acc = _dot(p_scr[buf, rows, nfull * t:(nfull + 1) * t], vx_scr[last_k:last_k + t, :])
            if nfull:
                acc = acc + _dot(p_scr[buf, rows, 0:nfull * t], vx_scr[0:nfull * t, :])
            o.append(acc[:, 0:DIFF_V_DIM] / acc[:, DIFF_V_DIM:])
        a = o[0] - lam * o[1]
        ms = jnp.mean(a * a, axis=1, keepdims=True)
        y = a * lax.rsqrt(ms + EPS) * out_gain * z_ref[0, qstart:qstart + n, :].astype(F32)
        out_ref[0, qstart:qstart + n, :] = y.astype(BF16)

    scores(0)
    for bi in range(len(blocks)):
        if bi + 1 < len(blocks):
            scores(bi + 1)
        softmax(bi)
        finish(bi)


def _attn(proj3, lamv, g_diff3, lam_init):
    bsz, seq, _ = proj3.shape
    assert seq >= ATTN_T and (seq % ATTN_T) % BF16_ROWS == 0 and (2 * (seq % ATTN_T)) % (2 * BF16_ROWS) == 0
    w = DIFF_V_DIM
    width = -(-seq // ATTN_T) * ATTN_T
    per_sec = SECTION // w

    def sec(s):
        return pl.BlockSpec((1, seq, w), lambda b, h, s=s: (b, 0, s * per_sec + h))

    return pl.pallas_call(
        functools.partial(_attn_kernel, seq=seq, lam_init=lam_init),
        grid=(bsz, DIFF_HEADS),
        in_specs=[
            sec(SEC_QD), sec(SEC_KD), sec(SEC_VD), sec(SEC_ZD),
            pl.BlockSpec((4, DIFF_HEAD_DIM), lambda b, h: (0, 0)),
            pl.BlockSpec((1, 1, w), lambda b, h: (h, 0, 0)),
        ],
        out_specs=pl.BlockSpec((1, seq, w), lambda b, h: (b, 0, h)),
        out_shape=jax.ShapeDtypeStruct((bsz, seq, D_DIFF), BF16),
        scratch_shapes=[
            pltpu.VMEM((2, 2 * (ATTN_T + seq % ATTN_T), width), F32),
            pltpu.VMEM((2, 2 * (ATTN_T + seq % ATTN_T), width), BF16),
            pltpu.VMEM((2 * ATTN_T, ATTN_T), F32),
            pltpu.VMEM((seq, 2 * w), BF16),
        ],
        compiler_params=pltpu.CompilerParams(
            dimension_semantics=("arbitrary", "arbitrary"), vmem_limit_bytes=VMEM_LIMIT),
        name="diffattn",
    )(proj3, proj3, proj3, proj3, lamv, g_diff3)


def _outproj_body(hm_ref, hd_ref, h_ref, wo_ref, gpost_ref):
    y = _dot(hm_ref[...], wo_ref[:D_MLSTM, :]) + _dot(hd_ref[...], wo_ref[D_MLSTM:, :])
    ms = jnp.mean(y * y, axis=-1, keepdims=True)
    return h_ref[...] + y * lax.rsqrt(ms + EPS) * gpost_ref[...]


def _outproj_next_kernel(hm_ref, hd_ref, h_ref, wo_ref, gpost_ref, gpre_ref, wg_ref, bg_ref,
                         hnew_ref, hn_ref, gates_ref):
    hnew = _outproj_body(hm_ref, hd_ref, h_ref, wo_ref, gpost_ref)
    hnew_ref[...] = hnew
    hn, gates = _prenorm_gates(hnew, gpre_ref[...], wg_ref[...], bg_ref[...])
    hn_ref[...] = hn
    gates_ref[...] = gates


def _outproj_last_kernel(hm_ref, hd_ref, h_ref, wo_ref, gpost_ref, hnew_ref):
    y = _dot(hm_ref[0], wo_ref[:D_MLSTM, :]) + _dot(hd_ref[0], wo_ref[D_MLSTM:, :])
    ms = jnp.mean(y * y, axis=-1, keepdims=True)
    hnew_ref[0] = h_ref[0] + y * lax.rsqrt(ms + EPS) * gpost_ref[...]


def _outproj(hm, hd, h, wo, gpost, nxt, nxt_shape=None):
    rows = h.shape[0]
    tm = _row_tile(rows)
    row_spec = lambda width: pl.BlockSpec((tm, width), lambda i: (i, 0))
    const = lambda shape: pl.BlockSpec(shape, lambda i: (0, 0))
    in_specs = [row_spec(D_MLSTM), row_spec(D_DIFF), row_spec(D_MODEL),
                const((D_MLSTM + D_DIFF, D_MODEL)), const((1, D_MODEL))]
    params = pltpu.CompilerParams(dimension_semantics=("arbitrary",), vmem_limit_bytes=VMEM_LIMIT)
    if nxt is None:
        bsz, s_real = nxt_shape
        seq = rows // bsz
        tr = _row_tile(s_real, cap=512)

        def real_rows(width):
            return pl.BlockSpec((pl.Element(1), pl.Element(tr), pl.Element(width)),
                                lambda b, i: (b, pl.multiple_of(N_META + i * tr, BF16_ROWS), 0))

        const2 = lambda shape: pl.BlockSpec(shape, lambda b, i: (0, 0))
        return pl.pallas_call(
            _outproj_last_kernel,
            grid=(bsz, s_real // tr),
            in_specs=[real_rows(D_MLSTM), real_rows(D_DIFF), real_rows(D_MODEL),
                      const2((D_MLSTM + D_DIFF, D_MODEL)), const2((1, D_MODEL))],
            out_specs=pl.BlockSpec((1, tr, D_MODEL), lambda b, i: (b, i, 0)),
            out_shape=jax.ShapeDtypeStruct((bsz, s_real, D_MODEL), F32),
            compiler_params=pltpu.CompilerParams(
                dimension_semantics=("arbitrary", "arbitrary"), vmem_limit_bytes=VMEM_LIMIT),
            name="outproj_last",
        )(hm.reshape(bsz, seq, D_MLSTM), hd.reshape(bsz, seq, D_DIFF), h.reshape(bsz, seq, D_MODEL),
          wo, gpost)
    gpre, wg, bg = nxt
    return pl.pallas_call(
        _outproj_next_kernel,
        grid=(rows // tm,),
        in_specs=in_specs + [const((1, D_MODEL)), const((D_MODEL, GATE_PAD)), const((1, GATE_PAD))],
        out_specs=[row_spec(D_MODEL), row_spec(D_MODEL), row_spec(GATE_PAD)],
        out_shape=[
            jax.ShapeDtypeStruct((rows, D_MODEL), F32),
            jax.ShapeDtypeStruct((rows, D_MODEL), BF16),
            jax.ShapeDtypeStruct((rows, GATE_PAD), F32),
        ],
        compiler_params=params,
        name="outproj",
    )(hm, hd, h, wo, gpost, gpre, wg, bg)


def kernel(x, meta_tokens, pre_norm_g, post_norm_g, w_in, b_gates, conv_w, conv_b, mlstm_norm_g,
           lambda_q1, lambda_k1, lambda_q2, lambda_k2, diff_norm_g, w_out):
    bsz, s_real, _ = x.shape
    depth = w_in.shape[0]
    seq = s_real + N_META
    rows = bsz * seq

    g0 = 5 * D_MLSTM
    w_m = w_in[:, :, :g0].astype(BF16)
    w_d = w_in[:, :, g0 + N_GATES:].astype(BF16)
    w_gate = jnp.pad(w_in[:, :, g0:g0 + N_GATES],
                     ((0, 0), (0, 0), (0, GATE_PAD - N_GATES))).astype(BF16)
    b_gate = jnp.pad(b_gates, ((0, 0), (0, GATE_PAD - N_GATES)))[:, None, :]
    w_o = w_out.astype(BF16)
    lamv = jnp.stack([lambda_q1, lambda_k1, lambda_q2, lambda_k2], axis=1)
    g_diff = diff_norm_g.reshape(depth, DIFF_HEADS, 1, DIFF_V_DIM)

    meta = jnp.broadcast_to(meta_tokens[None].astype(x.dtype), (bsz, N_META, D_MODEL))
    h = jnp.concatenate([meta, x], axis=1).reshape(rows, D_MODEL)

    hn, gates = _prenorm(h, pre_norm_g[0][None], w_gate[0], b_gate[0])
    for layer in range(depth):
        lam_init = 0.8 - 0.6 * math.exp(-0.3 * layer)
        proj3 = _inproj(hn.reshape(bsz, seq, D_MODEL), w_m[layer], w_d[layer], conv_w[layer],
                        conv_b[layer][None])
        hm = _mlstm(proj3, gates, mlstm_norm_g[layer][None])
        hd = _attn(proj3, lamv[layer], g_diff[layer], lam_init)
        hm = hm.reshape(rows, D_MLSTM)
        hd = hd.reshape(rows, D_DIFF)
        if layer + 1 < depth:
            nxt = (pre_norm_g[layer + 1][None], w_gate[layer + 1], b_gate[layer + 1])
            h, hn, gates = _outproj(hm, hd, h, w_o[layer], post_norm_g[layer][None], nxt)
        else:
            out = _outproj(hm, hd, h, w_o[layer], post_norm_g[layer][None], None, (bsz, s_real))
    return out
```
